```python
import jax, jax.numpy as jnp
from jax import lax
import numpy as np

D_MODEL = 1024
BATCH = 4
SEQ = 4096
DEPTH = 4

CHUNK = 64
N_LEFT_CHUNKS = 8
BAND_CHUNKS = N_LEFT_CHUNKS + 1
BAND = BAND_CHUNKS * CHUNK
LEFT_PAD = N_LEFT_CHUNKS * CHUNK
MIX_WIDTH = D_MODEL
CONV_WIDTH = MIX_WIDTH // 2
ATTN_WIDTH = MIX_WIDTH - CONV_WIDTH
HEAD_DIM = 64
N_HEADS = ATTN_WIDTH // HEAD_DIM
CONV_GROUPS = 8
CONV_K = 3
REL_CLIP = 128
D_FF = -(-8 * D_MODEL // (3 * 256)) * 256
PROJ_WIDTH = 3 * CONV_WIDTH + 3 * ATTN_WIDTH
EPS = 1e-6
NEG_INF = -1e30

kernel_name = "hybrid_shortconv_chunkattn_sandwich_trunk"


def rms_norm(x, g):
    xf = x.astype(jnp.float32)
    y = xf * lax.rsqrt(jnp.mean(xf * xf, axis=-1, keepdims=True) + EPS)
    return (y * g.astype(jnp.float32)).astype(x.dtype)


def group_rms_norm(y, g, n_groups):
    b, s, w = y.shape
    yf = y.astype(jnp.float32).reshape(b, s, n_groups, w // n_groups)
    yf = yf * lax.rsqrt(jnp.mean(yf * yf, axis=-1, keepdims=True) + EPS)
    return (yf.reshape(b, s, w) * g.astype(jnp.float32)).astype(y.dtype)


def short_gated_conv(h, b_gate, c_gate, w_conv):
    s = h.shape[1]
    u = c_gate * h
    up = jnp.pad(u, ((0, 0), (CONV_K - 1, 0), (0, 0)))
    out = up[:, 0:s] * w_conv[:, 0]
    for k in range(1, CONV_K):
        out = out + up[:, k:k + s] * w_conv[:, k]
    return b_gate * out


def chunked_band_attention(q, k, v, rel_bias):
    b, s, _ = q.shape
    nc = s // CHUNK
    qc = q.reshape(b, nc, CHUNK, N_HEADS, HEAD_DIM) * (HEAD_DIM ** -0.5)
    kp = jnp.pad(k, ((0, 0), (LEFT_PAD, 0), (0, 0))).reshape(b, nc + N_LEFT_CHUNKS, CHUNK, N_HEADS, HEAD_DIM)
    vp = jnp.pad(v, ((0, 0), (LEFT_PAD, 0), (0, 0))).reshape(b, nc + N_LEFT_CHUNKS, CHUNK, N_HEADS, HEAD_DIM)
    kb = jnp.concatenate([kp[:, o:o + nc] for o in range(BAND_CHUNKS)], axis=2)
    vb = jnp.concatenate([vp[:, o:o + nc] for o in range(BAND_CHUNKS)], axis=2)
    scores = jnp.einsum('bcqhd,bckhd->bhcqk', qc, kb).astype(jnp.float32)
    qi = jnp.arange(CHUNK)[:, None]
    kj = jnp.arange(BAND)[None, :]
    rel_idx = jnp.clip(qi - kj + LEFT_PAD, -REL_CLIP, REL_CLIP) + REL_CLIP
    bias = rel_bias.astype(jnp.float32)[:, rel_idx]
    key_pos = jnp.arange(nc)[:, None] * CHUNK - LEFT_PAD + jnp.arange(BAND)[None, :]
    valid = key_pos >= 0
    scores = jnp.where(valid[None, None, :, None, :], scores + bias[None, :, None], NEG_INF)
    p = jax.nn.softmax(scores, axis=-1).astype(v.dtype)
    o = jnp.einsum('bhcqk,bckhd->bcqhd', p, vb)
    return o.reshape(b, s, ATTN_WIDTH)


def setup_inputs(seed: int = 0) -> dict:
    key = jax.random.key(seed)
    ks = jax.random.split(key, 14)
    f32 = jnp.float32

    def gain(k, n):
        return 1.0 + 0.1 * jax.random.normal(k, (DEPTH, n), f32)

    return {
        "x": jax.random.normal(ks[0], (BATCH, SEQ, D_MODEL), f32),
        "w_in": jax.random.normal(ks[1], (DEPTH, D_MODEL, PROJ_WIDTH), f32) * D_MODEL ** -0.5,
        "w_conv": jax.random.normal(ks[2], (DEPTH, CONV_WIDTH, CONV_K), f32) * CONV_K ** -0.5,
        "rel_bias": 0.5 * jax.random.normal(ks[3], (DEPTH, N_HEADS, 2 * REL_CLIP + 1), f32),
        "g_conv_out": gain(ks[4], CONV_WIDTH),
        "g_attn_out": gain(ks[5], ATTN_WIDTH),
        "w_out": jax.random.normal(ks[6], (DEPTH, MIX_WIDTH, D_MODEL), f32) * MIX_WIDTH ** -0.5,
        "g_pre_mix": gain(ks[7], D_MODEL),
        "g_post_mix": gain(ks[8], D_MODEL),
        "g_pre_ffn": gain(ks[9], D_MODEL),
        "g_post_ffn": gain(ks[10], D_MODEL),
        "w_ffn_in": jax.random.normal(ks[11], (DEPTH, D_MODEL, 2 * D_FF), f32) * D_MODEL ** -0.5,
        "w_ffn_out": jax.random.normal(ks[12], (DEPTH, D_FF, D_MODEL), f32) * D_FF ** -0.5,
    }


def reference(x, w_in, w_conv, rel_bias, g_conv_out, g_attn_out, w_out,
              g_pre_mix, g_post_mix, g_pre_ffn, g_post_ffn, w_ffn_in, w_ffn_out):
    for l in range(DEPTH):
        h = rms_norm(x, g_pre_mix[l])
        proj = jnp.einsum('bsd,dp->bsp', h, w_in[l])
        hc, bg, cg, q, k, v = jnp.split(proj, 6, axis=-1)
        y_conv = short_gated_conv(hc, bg, cg, w_conv[l])
        y_attn = chunked_band_attention(q, k, v, rel_bias[l])
        y = jnp.concatenate([group_rms_norm(y_conv, g_conv_out[l], CONV_GROUPS),
                             group_rms_norm(y_attn, g_attn_out[l], N_HEADS)], axis=-1)
        y = jnp.einsum('bsm,md->bsd', y, w_out[l])
        x = x + rms_norm(y, g_post_mix[l])
        h = rms_norm(x, g_pre_ffn[l])
        gate, up = jnp.split(jnp.einsum('bsd,df->bsf', h, w_ffn_in[l]), 2, axis=-1)
        f = jnp.einsum('bsf,fd->bsd', jax.nn.silu(gate) * up, w_ffn_out[l])
        x = x + rms_norm(f, g_post_ffn[l])
    return x
```

```python
import functools

import jax
import jax.numpy as jnp
from jax import lax
from jax.experimental import pallas as pl
from jax.experimental.pallas import tpu as pltpu

D_MODEL = 1024
CHUNK = 64
N_LEFT_CHUNKS = 8
LEFT_PAD = N_LEFT_CHUNKS * CHUNK
CONV_WIDTH = 512
ATTN_WIDTH = 512
HEAD_DIM = 64
N_HEADS = 8
CONV_K = 3
REL_CLIP = 128
D_FF = 2816
EPS = 1e-6
NEG_INF = -1e30

LANES = 128
TM_PROJ = 512
TM_FFN = 512
TQ = 256
N_KBLK = LEFT_PAD // TQ + 1
NK = N_KBLK * TQ
FF_CHUNK = 256
VMEM_LIMIT = 56 * 1024 * 1024

F32 = jnp.float32
BF16 = jnp.bfloat16


def _rms(x, g):
    return (x * lax.rsqrt(jnp.mean(x * x, axis=-1, keepdims=True) + EPS)) * g


def _pair_group_norm(y, g):
    lane = lax.broadcasted_iota(jnp.int32, y.shape, 1)
    lo = lane < HEAD_DIM
    sq = y * y
    ms_lo = jnp.sum(jnp.where(lo, sq, 0.0), axis=-1, keepdims=True) * (1.0 / HEAD_DIM)
    ms_hi = jnp.sum(jnp.where(lo, 0.0, sq), axis=-1, keepdims=True) * (1.0 / HEAD_DIM)
    inv = jnp.where(lo, lax.rsqrt(ms_lo + EPS), lax.rsqrt(ms_hi + EPS))
    return (y * inv) * g


def _inproj_kernel(x_ref, g_ref, w_ref, wc_ref, gco_ref, yc_ref, q_ref, k_ref, v_ref, carry_ref):
    @pl.when(pl.program_id(1) == 0)
    def _():
        carry_ref[...] = jnp.zeros_like(carry_ref)

    tm = x_ref.shape[1]
    hb = _rms(x_ref[0], g_ref[...]).astype(BF16)

    def proj(j):
        return jnp.dot(hb, w_ref[:, j * CONV_WIDTH:(j + 1) * CONV_WIDTH], preferred_element_type=F32)

    u = proj(2) * proj(0)
    prev = carry_ref[...]
    rows = lax.broadcasted_iota(jnp.int32, (tm, 1), 0)
    u1 = jnp.where(rows == 0, prev[7:8], pltpu.roll(u, 1, 0))
    u2 = jnp.where(rows == 0, prev[6:7], jnp.where(rows == 1, prev[7:8], pltpu.roll(u, 2, 0)))
    carry_ref[...] = u[tm - 8:tm]
    conv = proj(1) * (u2 * wc_ref[0:1] + u1 * wc_ref[1:2] + u * wc_ref[2:3])
    for t in range(CONV_WIDTH // LANES):
        sl = slice(t * LANES, (t + 1) * LANES)
        yc_ref[0, :, sl] = _pair_group_norm(conv[:, sl], gco_ref[:, sl]).astype(BF16)

    q_ref[0] = (proj(3) * (HEAD_DIM ** -0.5)).astype(BF16)
    k_ref[0] = proj(4).astype(BF16)
    v_ref[0] = proj(5).astype(BF16)


def _inproj(x, g, w, wc, gco):
    b, s, d = x.shape
    tm = TM_PROJ
    const = lambda bi, i: (0, 0)
    row = lambda bi, i: (bi, i, 0)
    out = jax.ShapeDtypeStruct((b, s, CONV_WIDTH), BF16)
    return pl.pallas_call(
        _inproj_kernel,
        grid=(b, s // tm),
        in_specs=[
            pl.BlockSpec((1, tm, d), row),
            pl.BlockSpec((1, d), const),
            pl.BlockSpec(w.shape, const, pipeline_mode=pl.Buffered(1)),
            pl.BlockSpec(wc.shape, const),
            pl.BlockSpec(gco.shape, const),
        ],
        out_specs=[pl.BlockSpec((1, tm, CONV_WIDTH), row)] * 4,
        out_shape=[out] * 4,
        scratch_shapes=[pltpu.VMEM((8, CONV_WIDTH), F32)],
        compiler_params=pltpu.CompilerParams(
            dimension_semantics=("arbitrary", "arbitrary"), vmem_limit_bytes=VMEM_LIMIT),
        name="inproj_conv",
    )(x, g, w, wc, gco)


def _table_kernel(f_ref, o_ref):
    f = jnp.broadcast_to(f_ref[0], (TQ, f_ref.shape[-1]))
    r = pltpu.roll(f, 0, 1, stride=1, stride_axis=0)[:, :NK]
    qc = lax.broadcasted_iota(jnp.int32, (TQ, NK), 0) // CHUNK
    kc = lax.broadcasted_iota(jnp.int32, (TQ, NK), 1) // CHUNK
    valid = (kc >= qc) & (kc <= qc + N_LEFT_CHUNKS)
    o_ref[0] = jnp.where(valid, r, NEG_INF)


def _bias_tables(rel_bias):
    depth, h, _ = rel_bias.shape
    width = 1024
    m = jnp.arange(width)
    m = jnp.where(m < NK, m, m - width)
    idx = jnp.clip(LEFT_PAD - m, -REL_CLIP, REL_CLIP) + REL_CLIP
    f = rel_bias.astype(F32)[:, :, idx].reshape(depth * h, 1, width)
    return pl.pallas_call(
        _table_kernel,
        grid=(depth * h,),
        in_specs=[pl.BlockSpec((1, 1, width), lambda i: (i, 0, 0))],
        out_specs=pl.BlockSpec((1, TQ, NK), lambda i: (i, 0, 0)),
        out_shape=jax.ShapeDtypeStruct((depth * h, TQ, NK), F32),
        name="bias_tables",
    )(f)


def _attn_kernel(q_ref, k0_ref, k1_ref, k2_ref, v0_ref, v1_ref, v2_ref, tab_ref, g_ref, o_ref):
    i = pl.program_id(1)
    k_refs = (k0_ref, k1_ref, k2_ref)
    v_refs = (v0_ref, v1_ref, v2_ref)
    pen = [jnp.where(i >= N_KBLK - 1 - j, 0.0, NEG_INF).astype(F32) for j in range(N_KBLK - 1)] + [None]
    lane_q = lax.broadcasted_iota(jnp.int32, (TQ, LANES), 1) < HEAD_DIM
    lane_v = lax.broadcasted_iota(jnp.int32, (TQ, LANES), 1) < HEAD_DIM
    nt = (((1,), (1,)), ((), ()))
    for pair in range(N_HEADS // 2):
        sl = slice(pair * LANES, (pair + 1) * LANES)
        qp = q_ref[0, :, sl]
        acc = None
        denom = []
        for half in range(2):
            head = 2 * pair + half
            sel_q = lane_q if half == 0 else jnp.logical_not(lane_q)
            sel_v = lane_v if half == 0 else jnp.logical_not(lane_v)
            qm = jnp.where(sel_q, qp, jnp.zeros_like(qp))
            s = []
            for j in range(N_KBLK):
                sj = lax.dot_general(qm, k_refs[j][0, :, sl], nt, preferred_element_type=F32)
                sj = sj + tab_ref[head, :, j * TQ:(j + 1) * TQ]
                if pen[j] is not None:
                    sj = sj + pen[j]
                s.append(sj)
            mx = jnp.max(functools.reduce(jnp.maximum, s), axis=-1, keepdims=True)
            p = [jnp.exp(sj - mx) for sj in s]
            denom.append(jnp.sum(functools.reduce(jnp.add, p), axis=-1, keepdims=True))
            for j in range(N_KBLK):
                vj = v_refs[j][0, :, sl]
                vm = jnp.where(sel_v, vj, jnp.zeros_like(vj))
                c = jnp.dot(p[j].astype(BF16), vm, preferred_element_type=F32)
                acc = c if acc is None else acc + c
        y = acc / jnp.where(lane_q, denom[0], denom[1])
        o_ref[0, :, sl] = _pair_group_norm(y, g_ref[:, sl]).astype(BF16)


def _attention(q, k, v, tab, g):
    b, s, w = q.shape
    blk = (1, TQ, w)
    cur = lambda bi, i: (bi, i, 0)

    def back(n):
        return lambda bi, i: (bi, jnp.maximum(i - n, 0), 0)

    kv_specs = [pl.BlockSpec(blk, back(N_KBLK - 1 - j)) for j in range(N_KBLK)]
    return pl.pallas_call(
        _attn_kernel,
        grid=(b, s // TQ),
        in_specs=[pl.BlockSpec(blk, cur)] + kv_specs + kv_specs + [
            pl.BlockSpec(tab.shape, lambda bi, i: (0, 0, 0), pipeline_mode=pl.Buffered(1)),
            pl.BlockSpec(g.shape, lambda bi, i: (0, 0)),
        ],
        out_specs=pl.BlockSpec(blk, cur),
        out_shape=jax.ShapeDtypeStruct((b, s, w), BF16),
        compiler_params=pltpu.CompilerParams(
            dimension_semantics=("arbitrary", "arbitrary"), vmem_limit_bytes=VMEM_LIMIT),
        name="band_attention",
    )(q, k, k, k, v, v, v, tab, g)


def _ffn_kernel(x_ref, yc_ref, ya_ref, wo_ref, gpm_ref, gpf_ref, wg_ref, wu_ref, wd_ref, gqf_ref, o_ref):
    x = x_ref[...]
    y = jnp.dot(yc_ref[...], wo_ref[0:CONV_WIDTH], preferred_element_type=F32)
    y = y + jnp.dot(ya_ref[...], wo_ref[CONV_WIDTH:], preferred_element_type=F32)
    x = x + _rms(y, gpm_ref[...])
    hb = _rms(x, gpf_ref[...]).astype(BF16)
    f = None
    for c in range(D_FF // FF_CHUNK):
        sl = slice(c * FF_CHUNK, (c + 1) * FF_CHUNK)
        gate = jnp.dot(hb, wg_ref[:, sl], preferred_element_type=F32)
        up = jnp.dot(hb, wu_ref[:, sl], preferred_element_type=F32)
        a = ((gate * jax.nn.sigmoid(gate)) * up).astype(BF16)
        d = jnp.dot(a, wd_ref[sl, :], preferred_element_type=F32)
        f = d if f is None else f + d
    o_ref[...] = x + _rms(f, gqf_ref[...])


def _outproj_ffn(x, yc, ya, wo, gpm, gpf, wg, wu, wd, gqf):
    t, d = x.shape
    tm = TM_FFN
    const = lambda i: (0, 0)
    row = lambda i: (i, 0)

    def weight(a):
        return pl.BlockSpec(a.shape, const, pipeline_mode=pl.Buffered(1))

    gain = pl.BlockSpec((1, d), const)
    return pl.pallas_call(
        _ffn_kernel,
        grid=(t // tm,),
        in_specs=[
            pl.BlockSpec((tm, d), row),
            pl.BlockSpec((tm, CONV_WIDTH), row),
            pl.BlockSpec((tm, ATTN_WIDTH), row),
            weight(wo), gain, gain, weight(wg), weight(wu), weight(wd), gain,
        ],
        out_specs=pl.BlockSpec((tm, d), row),
        out_shape=jax.ShapeDtypeStruct((t, d), F32),
        compiler_params=pltpu.CompilerParams(
            dimension_semantics=("arbitrary",), vmem_limit_bytes=VMEM_LIMIT),
        name="outproj_ffn",
    )(x, yc, ya, wo, gpm, gpf, wg, wu, wd, gqf)


def kernel(x, w_in, w_conv, rel_bias, g_conv_out, g_attn_out, w_out, g_pre_mix, g_post_mix,
           g_pre_ffn, g_post_ffn, w_ffn_in, w_ffn_out):
    b, s, d = x.shape
    depth = w_in.shape[0]
    tabs = _bias_tables(rel_bias).reshape(depth, N_HEADS, TQ, NK)
    w_in_b = w_in.astype(BF16)
    w_out_b = w_out.astype(BF16)
    w_gate_b = w_ffn_in[:, :, :D_FF].astype(BF16)
    w_up_b = w_ffn_in[:, :, D_FF:].astype(BF16)
    w_down_b = w_ffn_out.astype(BF16)
    w_conv_t = jnp.swapaxes(w_conv, 1, 2)
    for l in range(depth):
        yc, q, k, v = _inproj(x, g_pre_mix[l][None], w_in_b[l], w_conv_t[l], g_conv_out[l][None])
        ya = _attention(q, k, v, tabs[l], g_attn_out[l][None])
        x = _outproj_ffn(
            x.reshape(b * s, d), yc.reshape(b * s, CONV_WIDTH), ya.reshape(b * s, ATTN_WIDTH),
            w_out_b[l], g_post_mix[l][None], g_pre_ffn[l][None],
            w_gate_b[l], w_up_b[l], w_down_b[l], g_post_ffn[l][None]).reshape(b, s, d)
    return x
```

```python
import math

import jax
import jax.numpy as jnp
from jax import lax
from jax.experimental import pallas as pl
from jax.experimental.pallas import tpu as pltpu

D_MODEL = 1024
CHUNK = 64
N_LEFT_CHUNKS = 8
LEFT_PAD = N_LEFT_CHUNKS * CHUNK
CONV_WIDTH = 512
ATTN_WIDTH = 512
HEAD_DIM = 64
N_HEADS = 8
CONV_K = 3
REL_CLIP = 128
D_FF = 2816
EPS = 1e-6
NEG_INF = -1e30
LOG2E = math.log2(math.e)

LANES = 128
TM_PROJ = 512
TM_FFN = 512
TQ = 256
N_KBLK = LEFT_PAD // TQ + 1
NK = N_KBLK * TQ
SCORE_LEAD = 2
TABLE_WIDTH = 1024
FF_CHUNK = 256
VMEM_LIMIT = 56 * 1024 * 1024

F32 = jnp.float32
BF16 = jnp.bfloat16


def _rms(x, g):
    return (x * lax.rsqrt(jnp.mean(x * x, axis=-1, keepdims=True) + EPS)) * g


def _pair_group_norm(y, g):
    lane = lax.broadcasted_iota(jnp.int32, y.shape, 1)
    lo = lane < HEAD_DIM
    sq = y * y
    ms_lo = jnp.sum(jnp.where(lo, sq, 0.0), axis=-1, keepdims=True) * (1.0 / HEAD_DIM)
    ms_hi = jnp.sum(jnp.where(lo, 0.0, sq), axis=-1, keepdims=True) * (1.0 / HEAD_DIM)
    inv = jnp.where(lo, lax.rsqrt(ms_lo + EPS), lax.rsqrt(ms_hi + EPS))
    return (y * inv) * g


def _layer_spec(a, layer, n_grid, single_buffer=False):
    zeros = (0,) * (a.ndim - 1)
    index = {1: lambda i: (layer,) + zeros, 2: lambda bi, i: (layer,) + zeros}[n_grid]
    mode = pl.Buffered(1) if single_buffer else None
    return pl.BlockSpec((None,) + a.shape[1:], index, pipeline_mode=mode)


def _inproj_kernel(x_ref, g_ref, w_ref, wc_ref, gco_ref, yc_ref, q_ref, k_ref, v_ref, carry_ref):
    @pl.when(pl.program_id(1) == 0)
    def _():
        carry_ref[...] = jnp.zeros_like(carry_ref)

    tm = x_ref.shape[1]
    hb = _rms(x_ref[0], g_ref[...]).astype(BF16)

    def proj(j):
        return jnp.dot(hb, w_ref[:, j * CONV_WIDTH:(j + 1) * CONV_WIDTH], preferred_element_type=F32)

    u = proj(2) * proj(0)
    prev = carry_ref[...]
    rows = lax.broadcasted_iota(jnp.int32, (tm, 1), 0)
    u1 = jnp.where(rows == 0, prev[7:8], pltpu.roll(u, 1, 0))
    u2 = jnp.where(rows == 0, prev[6:7], jnp.where(rows == 1, prev[7:8], pltpu.roll(u, 2, 0)))
    carry_ref[...] = u[tm - 8:tm]
    conv = proj(1) * (u2 * wc_ref[0:1] + u1 * wc_ref[1:2] + u * wc_ref[2:3])
    for t in range(CONV_WIDTH // LANES):
        sl = slice(t * LANES, (t + 1) * LANES)
        yc_ref[0, :, sl] = _pair_group_norm(conv[:, sl], gco_ref[:, sl]).astype(BF16)

    q = proj(3) * (LOG2E * HEAD_DIM ** -0.5)
    k_ref[0] = proj(4).astype(BF16)
    v = proj(5)
    lo = lax.broadcasted_iota(jnp.int32, (tm, LANES), 1) < HEAD_DIM
    for pair in range(N_HEADS // 2):
        sl = slice(pair * LANES, (pair + 1) * LANES)
        first = slice(2 * pair * LANES, (2 * pair + 1) * LANES)
        second = slice((2 * pair + 1) * LANES, (2 * pair + 2) * LANES)
        q_ref[0, :, first] = jnp.where(lo, q[:, sl], 0.0).astype(BF16)
        q_ref[0, :, second] = jnp.where(lo, 0.0, q[:, sl]).astype(BF16)
        v_ref[0, :, first] = jnp.where(lo, v[:, sl], 1.0).astype(BF16)
        v_ref[0, :, second] = jnp.where(lo, 1.0, v[:, sl]).astype(BF16)


def _inproj(x, layer, g, w, wc, gco):
    b, s, d = x.shape
    tm = TM_PROJ
    row = lambda bi, i: (bi, i, 0)
    narrow = jax.ShapeDtypeStruct((b, s, CONV_WIDTH), BF16)
    wide = jax.ShapeDtypeStruct((b, s, N_HEADS * LANES), BF16)
    narrow_spec = pl.BlockSpec((1, tm, CONV_WIDTH), row)
    wide_spec = pl.BlockSpec((1, tm, N_HEADS * LANES), row)
    return pl.pallas_call(
        _inproj_kernel,
        grid=(b, s // tm),
        in_specs=[
            pl.BlockSpec((1, tm, d), row),
            _layer_spec(g, layer, 2),
            _layer_spec(w, layer, 2, single_buffer=True),
            _layer_spec(wc, layer, 2),
            _layer_spec(gco, layer, 2),
        ],
        out_specs=[narrow_spec, wide_spec, narrow_spec, wide_spec],
        out_shape=[narrow, wide, narrow, wide],
        scratch_shapes=[pltpu.VMEM((8, CONV_WIDTH), F32)],
        compiler_params=pltpu.CompilerParams(
            dimension_semantics=("arbitrary", "arbitrary"), vmem_limit_bytes=VMEM_LIMIT),
        name="inproj_conv",
    )(x, g, w, wc, gco)


def _table_kernel(f_ref, o_ref):
    f = jnp.broadcast_to(f_ref[0], (TQ, TABLE_WIDTH))
    r = pltpu.roll(f, 0, 1, stride=1, stride_axis=0)[:, :NK]
    qc = lax.broadcasted_iota(jnp.int32, (TQ, NK), 0) // CHUNK
    kc = lax.broadcasted_iota(jnp.int32, (TQ, NK), 1) // CHUNK
    valid = (kc >= qc) & (kc <= qc + N_LEFT_CHUNKS)
    o_ref[0] = jnp.where(valid, r * LOG2E, NEG_INF)


def _bias_tables(rel_bias):
    depth, h, _ = rel_bias.shape
    m = jnp.arange(TABLE_WIDTH)
    m = jnp.where(m < NK, m, m - TABLE_WIDTH)
    idx = jnp.clip(LEFT_PAD - m, -REL_CLIP, REL_CLIP) + REL_CLIP
    f = rel_bias.astype(F32)[:, :, idx].reshape(depth * h, 1, TABLE_WIDTH)
    tabs = pl.pallas_call(
        _table_kernel,
        grid=(depth * h,),
        in_specs=[pl.BlockSpec((1, 1, TABLE_WIDTH), lambda i: (i, 0, 0))],
        out_specs=pl.BlockSpec((1, TQ, NK), lambda i: (i, 0, 0)),
        out_shape=jax.ShapeDtypeStruct((depth * h, TQ, NK), F32),
        name="bias_tables",
    )(f)
    return tabs.reshape(depth, h, TQ, NK)


def _attn_kernel(q_ref, k_ref, v_ref, tab_ref, g_ref, o_ref):
    i = pl.program_id(1)
    lo = lax.broadcasted_iota(jnp.int32, (TQ, LANES), 1) < HEAD_DIM
    nt = (((1,), (1,)), ((), ()))

    def attend(n_blk, start):
        n_keys = n_blk * TQ
        col0 = NK - n_keys
        def scores(head):
            pl_ = slice(head // 2 * LANES, (head // 2 + 1) * LANES)
            hl = slice(head * LANES, (head + 1) * LANES)
            s = lax.dot_general(q_ref[0, :, hl], k_ref[0, pl.ds(start, n_keys), pl_], nt,
                                preferred_element_type=F32)
            s = s + tab_ref[head, :, col0:]
            return s, jnp.max(s, axis=-1, keepdims=True)

        def weighted_values(head, s, mx):
            hl = slice(head * LANES, (head + 1) * LANES)
            p = jnp.exp2(s - mx).astype(BF16)
            return jnp.dot(p, v_ref[0, pl.ds(start, n_keys), hl], preferred_element_type=F32)

        pending = [scores(h) for h in range(SCORE_LEAD)]
        acc = []
        for head in range(N_HEADS):
            if head + SCORE_LEAD < N_HEADS:
                pending.append(scores(head + SCORE_LEAD))
            acc.append(weighted_values(head, *pending.pop(0)))
            if head % 2 == 1:
                sl = slice(head // 2 * LANES, (head // 2 + 1) * LANES)
                a0, a1 = acc
                acc = []
                y = jnp.where(lo, a0 / pltpu.roll(a0, HEAD_DIM, 1), a1 / pltpu.roll(a1, HEAD_DIM, 1))
                o_ref[0, :, sl] = _pair_group_norm(y, g_ref[:, sl]).astype(BF16)

    for n_blk in range(1, N_KBLK):
        @pl.when(i == n_blk - 1)
        def _(n_blk=n_blk):
            attend(n_blk, 0)

    @pl.when(i >= N_KBLK - 1)
    def _():
        attend(N_KBLK, pl.multiple_of((i - (N_KBLK - 1)) * TQ, TQ))


def _attention(q, k, v, layer, tab, g):
    b, s, w = k.shape
    cur = lambda bi, i: (bi, i, 0)
    seq = lambda bi, i: (bi, 0, 0)
    return pl.pallas_call(
        _attn_kernel,
        grid=(b, s // TQ),
        in_specs=[
            pl.BlockSpec((1, TQ, q.shape[-1]), cur),
            pl.BlockSpec((1, s, w), seq),
            pl.BlockSpec((1, s, v.shape[-1]), seq),
            _layer_spec(tab, layer, 2, single_buffer=True),
            _layer_spec(g, layer, 2),
        ],
        out_specs=pl.BlockSpec((1, TQ, w), cur),
        out_shape=jax.ShapeDtypeStruct((b, s, w), BF16),
        compiler_params=pltpu.CompilerParams(
            dimension_semantics=("arbitrary", "arbitrary"), vmem_limit_bytes=VMEM_LIMIT),
        name="band_attention",
    )(q, k, v, tab, g)


def _ffn_kernel(x_ref, yc_ref, ya_ref, wo_ref, gpm_ref, gpf_ref, wi_ref, wd_ref, gqf_ref, o_ref):
    x = x_ref[...]
    y = jnp.dot(yc_ref[...], wo_ref[0:CONV_WIDTH], preferred_element_type=F32)
    y = y + jnp.dot(ya_ref[...], wo_ref[CONV_WIDTH:], preferred_element_type=F32)
    x = x + _rms(y, gpm_ref[...])
    hb = _rms(x, gpf_ref[...]).astype(BF16)
    f = None
    for c in range(D_FF // FF_CHUNK):
        sl = slice(c * FF_CHUNK, (c + 1) * FF_CHUNK)
        sl_up = slice(D_FF + c * FF_CHUNK, D_FF + (c + 1) * FF_CHUNK)
        gate = jnp.dot(hb, wi_ref[:, sl], preferred_element_type=F32)
        up = jnp.dot(hb, wi_ref[:, sl_up], preferred_element_type=F32)
        a = ((gate * jax.nn.sigmoid(gate)) * up).astype(BF16)
        d = jnp.dot(a, wd_ref[sl, :], preferred_element_type=F32)
        f = d if f is None else f + d
    o_ref[...] = x + _rms(f, gqf_ref[...])


def _outproj_ffn(x, yc, ya, layer, wo, gpm, gpf, wi, wd, gqf):
    t, d = x.shape
    tm = TM_FFN
    row = lambda i: (i, 0)
    return pl.pallas_call(
        _ffn_kernel,
        grid=(t // tm,),
        in_specs=[
            pl.BlockSpec((tm, d), row),
            pl.BlockSpec((tm, CONV_WIDTH), row),
            pl.BlockSpec((tm, ATTN_WIDTH), row),
            _layer_spec(wo, layer, 1, single_buffer=True),
            _layer_spec(gpm, layer, 1),
            _layer_spec(gpf, layer, 1),
            _layer_spec(wi, layer, 1, single_buffer=True),
            _layer_spec(wd, layer, 1, single_buffer=True),
            _layer_spec(gqf, layer, 1),
        ],
        out_specs=pl.BlockSpec((tm, d), row),
        out_shape=jax.ShapeDtypeStruct((t, d), F32),
        compiler_params=pltpu.CompilerParams(
            dimension_semantics=("arbitrary",), vmem_limit_bytes=VMEM_LIMIT),
        name="outproj_ffn",
    )(x, yc, ya, wo, gpm, gpf, wi, wd, gqf)


def kernel(x, w_in, w_conv, rel_bias, g_conv_out, g_attn_out, w_out, g_pre_mix, g_post_mix,
           g_pre_ffn, g_post_ffn, w_ffn_in, w_ffn_out):
    b, s, d = x.shape
    depth = w_in.shape[0]
    tabs = _bias_tables(rel_bias)
    w_in_b = w_in.astype(BF16)
    w_out_b = w_out.astype(BF16)
    w_ffn_in_b = w_ffn_in.astype(BF16)
    w_ffn_out_b = w_ffn_out.astype(BF16)
    w_conv_t = jnp.swapaxes(w_conv, 1, 2)
    row3 = lambda a: a.reshape(depth, 1, a.shape[-1])
    g_conv_out, g_attn_out, g_pre_mix, g_post_mix, g_pre_ffn, g_post_ffn = map(
        row3, (g_conv_out, g_attn_out, g_pre_mix, g_post_mix, g_pre_ffn, g_post_ffn))
    for l in range(depth):
        yc, q, k, v = _inproj(x, l, g_pre_mix, w_in_b, w_conv_t, g_conv_out)
        ya = _attention(q, k, v, l, tabs, g_attn_out)
        x = _outproj_ffn(
            x.reshape(b * s, d), yc.reshape(b * s, CONV_WIDTH), ya.reshape(b * s, ATTN_WIDTH), l,
            w_out_b, g_post_mix, g_pre_ffn, w_ffn_in_b, w_ffn_out_b, g_post_ffn).reshape(b, s, d)
    return x
```

```python
import math

import jax
import jax.numpy as jnp
from jax import lax
from jax.experimental import pallas as pl
from jax.experimental.pallas import tpu as pltpu

D_MODEL = 1024
CHUNK = 64
N_LEFT_CHUNKS = 8
LEFT_PAD = N_LEFT_CHUNKS * CHUNK
CONV_WIDTH = 512
ATTN_WIDTH = 512
HEAD_DIM = 64
N_HEADS = 8
CONV_K = 3
REL_CLIP = 128
D_FF = 2816
EPS = 1e-6
NEG_INF = -1e30
LOG2E = math.log2(math.e)

LANES = 128
TM_PROJ = 1024
TM_FFN = 1024
SUB_ROWS = 512
TQ = 256
N_KBLK = LEFT_PAD // TQ + 1
NK = N_KBLK * TQ
SCORE_LEAD = 2
TABLE_WIDTH = 1024
FF_CHUNK = 256
VMEM_LIMIT = 56 * 1024 * 1024

F32 = jnp.float32
BF16 = jnp.bfloat16


def _rms(x, g):
    return (x * lax.rsqrt(jnp.mean(x * x, axis=-1, keepdims=True) + EPS)) * g


def _pair_group_norm(y, g):
    lane = lax.broadcasted_iota(jnp.int32, y.shape, 1)
    lo = lane < HEAD_DIM
    sq = y * y
    ms_lo = jnp.sum(jnp.where(lo, sq, 0.0), axis=-1, keepdims=True) * (1.0 / HEAD_DIM)
    ms_hi = jnp.sum(jnp.where(lo, 0.0, sq), axis=-1, keepdims=True) * (1.0 / HEAD_DIM)
    inv = jnp.where(lo, lax.rsqrt(ms_lo + EPS), lax.rsqrt(ms_hi + EPS))
    return (y * inv) * g


def _layer_spec(a, layer, n_grid, single_buffer=False):
    zeros = (0,) * (a.ndim - 1)
    index = {1: lambda i: (layer,) + zeros, 2: lambda bi, i: (layer,) + zeros}[n_grid]
    mode = pl.Buffered(1) if single_buffer else None
    return pl.BlockSpec((None,) + a.shape[1:], index, pipeline_mode=mode)


def _inproj_kernel(x_ref, g_ref, w_ref, wc_ref, gco_ref, yc_ref, q_ref, k_ref, v_ref, carry_ref):
    @pl.when(pl.program_id(1) == 0)
    def _():
        carry_ref[...] = jnp.zeros_like(carry_ref)

    n_sub = x_ref.shape[1] // SUB_ROWS
    lo = lax.broadcasted_iota(jnp.int32, (SUB_ROWS, LANES), 1) < HEAD_DIM
    row8 = lax.broadcasted_iota(jnp.int32, (8, 1), 0)

    def rows(t):
        return slice(t * SUB_ROWS, (t + 1) * SUB_ROWS)

    def project(hb, cols):
        return [jnp.dot(hb, w_ref[:, j * CONV_WIDTH:(j + 1) * CONV_WIDTH], preferred_element_type=F32)
                for j in cols]

    def conv_branch(t, h, b_gate, c_gate, prev):
        u = c_gate * h
        u1 = pltpu.roll(u, 1, 0)
        u2 = pltpu.roll(u, 2, 0)
        u1 = jnp.concatenate([jnp.where(row8 == 0, prev[7:8], u1[:8]), u1[8:]], axis=0)
        u2 = jnp.concatenate(
            [jnp.where(row8 == 0, prev[6:7], jnp.where(row8 == 1, prev[7:8], u2[:8])), u2[8:]], axis=0)
        conv = b_gate * (u2 * wc_ref[0:1] + u1 * wc_ref[1:2] + u * wc_ref[2:3])
        for j in range(CONV_WIDTH // LANES):
            sl = slice(j * LANES, (j + 1) * LANES)
            yc_ref[0, rows(t), sl] = _pair_group_norm(conv[:, sl], gco_ref[:, sl]).astype(BF16)
        return u[SUB_ROWS - 8:]

    def qkv(t, q, k, v):
        q = q * (LOG2E * HEAD_DIM ** -0.5)
        k_ref[0, rows(t), :] = k.astype(BF16)
        for pair in range(N_HEADS // 2):
            sl = slice(pair * LANES, (pair + 1) * LANES)
            first = slice(2 * pair * LANES, (2 * pair + 1) * LANES)
            second = slice((2 * pair + 1) * LANES, (2 * pair + 2) * LANES)
            q_ref[0, rows(t), first] = jnp.where(lo, q[:, sl], 0.0).astype(BF16)
            q_ref[0, rows(t), second] = jnp.where(lo, 0.0, q[:, sl]).astype(BF16)
            v_ref[0, rows(t), first] = jnp.where(lo, v[:, sl], 1.0).astype(BF16)
            v_ref[0, rows(t), second] = jnp.where(lo, 1.0, v[:, sl]).astype(BF16)

    hbs = [_rms(x_ref[0, rows(t), :], g_ref[...]).astype(BF16) for t in range(n_sub)]
    prev = carry_ref[...]
    proj = project(hbs[0], range(6))
    for t in range(n_sub):
        nxt = project(hbs[t + 1], range(1)) if t + 1 < n_sub else []
        prev = conv_branch(t, *proj[:3], prev)
        if nxt:
            nxt += project(hbs[t + 1], range(1, 3))
        qkv(t, *proj[3:])
        if nxt:
            nxt += project(hbs[t + 1], range(3, 6))
        proj = nxt
    carry_ref[...] = prev


def _inproj(x, layer, g, w, wc, gco):
    b, s, d = x.shape
    tm = TM_PROJ
    row = lambda bi, i: (bi, i, 0)
    narrow = jax.ShapeDtypeStruct((b, s, CONV_WIDTH), BF16)
    wide = jax.ShapeDtypeStruct((b, s, N_HEADS * LANES), BF16)
    narrow_spec = pl.BlockSpec((1, tm, CONV_WIDTH), row)
    wide_spec = pl.BlockSpec((1, tm, N_HEADS * LANES), row)
    return pl.pallas_call(
        _inproj_kernel,
        grid=(b, s // tm),
        in_specs=[
            pl.BlockSpec((1, tm, d), row),
            _layer_spec(g, layer, 2),
            _layer_spec(w, layer, 2, single_buffer=True),
            _layer_spec(wc, layer, 2),
            _layer_spec(gco, layer, 2),
        ],
        out_specs=[narrow_spec, wide_spec, narrow_spec, wide_spec],
        out_shape=[narrow, wide, narrow, wide],
        scratch_shapes=[pltpu.VMEM((8, CONV_WIDTH), F32)],
        compiler_params=pltpu.CompilerParams(
            dimension_semantics=("arbitrary", "arbitrary"), vmem_limit_bytes=VMEM_LIMIT),
        name="inproj_conv",
    )(x, g, w, wc, gco)


def _table_kernel(f_ref, o_ref):
    f = jnp.broadcast_to(f_ref[0], (TQ, TABLE_WIDTH))
    r = pltpu.roll(f, 0, 1, stride=1, stride_axis=0)[:, :NK]
    qc = lax.broadcasted_iota(jnp.int32, (TQ, NK), 0) // CHUNK
    kc = lax.broadcasted_iota(jnp.int32, (TQ, NK), 1) // CHUNK
    valid = (kc >= qc) & (kc <= qc + N_LEFT_CHUNKS)
    o_ref[0] = jnp.where(valid, r * LOG2E, NEG_INF)


def _bias_tables(rel_bias):
    depth, h, _ = rel_bias.shape
    m = jnp.arange(TABLE_WIDTH)
    m = jnp.where(m < NK, m, m - TABLE_WIDTH)
    idx = jnp.clip(LEFT_PAD - m, -REL_CLIP, REL_CLIP) + REL_CLIP
    f = rel_bias.astype(F32)[:, :, idx].reshape(depth * h, 1, TABLE_WIDTH)
    tabs = pl.pallas_call(
        _table_kernel,
        grid=(depth * h,),
        in_specs=[pl.BlockSpec((1, 1, TABLE_WIDTH), lambda i: (i, 0, 0))],
        out_specs=pl.BlockSpec((1, TQ, NK), lambda i: (i, 0, 0)),
        out_shape=jax.ShapeDtypeStruct((depth * h, TQ, NK), F32),
        name="bias_tables",
    )(f)
    return tabs.reshape(depth, h, TQ, NK)


def _attn_kernel(q_ref, k_ref, v_ref, tab_ref, g_ref, o_ref):
    i = pl.program_id(1)
    lo = lax.broadcasted_iota(jnp.int32, (TQ, LANES), 1) < HEAD_DIM
    nt = (((1,), (1,)), ((), ()))

    def attend(n_blk, start):
        n_keys = n_blk * TQ
        col0 = NK - n_keys
        def scores(head):
            pl_ = slice(head // 2 * LANES, (head // 2 + 1) * LANES)
            hl = slice(head * LANES, (head + 1) * LANES)
            s = lax.dot_general(q_ref[0, :, hl], k_ref[0, pl.ds(start, n_keys), pl_], nt,
                                preferred_element_type=F32)
            s = s + tab_ref[head, :, col0:]
            return s, jnp.max(s, axis=-1, keepdims=True)

        def weighted_values(head, s, mx):
            hl = slice(head * LANES, (head + 1) * LANES)
            p = jnp.exp2(s - mx).astype(BF16)
            return jnp.dot(p, v_ref[0, pl.ds(start, n_keys), hl], preferred_element_type=F32)

        pending = [scores(h) for h in range(SCORE_LEAD)]
        acc = []
        for head in range(N_HEADS):
            if head + SCORE_LEAD < N_HEADS:
                pending.append(scores(head + SCORE_LEAD))
            acc.append(weighted_values(head, *pending.pop(0)))
            if head % 2 == 1:
                sl = slice(head // 2 * LANES, (head // 2 + 1) * LANES)
                a0, a1 = acc
                acc = []
                y = jnp.where(lo, a0 / pltpu.roll(a0, HEAD_DIM, 1), a1 / pltpu.roll(a1, HEAD_DIM, 1))
                o_ref[0, :, sl] = _pair_group_norm(y, g_ref[:, sl]).astype(BF16)

    for n_blk in range(1, N_KBLK):
        @pl.when(i == n_blk - 1)
        def _(n_blk=n_blk):
            attend(n_blk, 0)

    @pl.when(i >= N_KBLK - 1)
    def _():
        attend(N_KBLK, pl.multiple_of((i - (N_KBLK - 1)) * TQ, TQ))


def _attention(q, k, v, layer, tab, g):
    b, s, w = k.shape
    cur = lambda bi, i: (bi, i, 0)
    seq = lambda bi, i: (bi, 0, 0)
    return pl.pallas_call(
        _attn_kernel,
        grid=(b, s // TQ),
        in_specs=[
            pl.BlockSpec((1, TQ, q.shape[-1]), cur),
            pl.BlockSpec((1, s, w), seq),
            pl.BlockSpec((1, s, v.shape[-1]), seq),
            _layer_spec(tab, layer, 2, single_buffer=True),
            _layer_spec(g, layer, 2),
        ],
        out_specs=pl.BlockSpec((1, TQ, w), cur),
        out_shape=jax.ShapeDtypeStruct((b, s, w), BF16),
        compiler_params=pltpu.CompilerParams(
            dimension_semantics=("arbitrary", "arbitrary"), vmem_limit_bytes=VMEM_LIMIT),
        name="band_attention",
    )(q, k, v, tab, g)


def _ffn_kernel(x_ref, yc_ref, ya_ref, wo_ref, gpm_ref, gpf_ref, wi_ref, wd_ref, gqf_ref, o_ref):
    n_sub = x_ref.shape[0] // SUB_ROWS
    n_chunks = D_FF // FF_CHUNK

    def rows(t):
        return slice(t * SUB_ROWS, (t + 1) * SUB_ROWS)

    def out_proj(t):
        y = jnp.dot(yc_ref[rows(t), :], wo_ref[0:CONV_WIDTH], preferred_element_type=F32)
        return y + jnp.dot(ya_ref[rows(t), :], wo_ref[CONV_WIDTH:], preferred_element_type=F32)

    def mid_norms(t, y):
        x = x_ref[rows(t), :] + _rms(y, gpm_ref[...])
        return x, _rms(x, gpf_ref[...]).astype(BF16)

    def ffn_chunk(hb, c):
        sl = slice(c * FF_CHUNK, (c + 1) * FF_CHUNK)
        sl_up = slice(D_FF + c * FF_CHUNK, D_FF + (c + 1) * FF_CHUNK)
        gate = jnp.dot(hb, wi_ref[:, sl], preferred_element_type=F32)
        up = jnp.dot(hb, wi_ref[:, sl_up], preferred_element_type=F32)
        a = ((gate * jax.nn.sigmoid(gate)) * up).astype(BF16)
        return jnp.dot(a, wd_ref[sl, :], preferred_element_type=F32)

    def finish(t, x, f):
        o_ref[rows(t), :] = x + _rms(f, gqf_ref[...])

    ys = [out_proj(t) for t in range(n_sub)]
    x, hb = mid_norms(0, ys[0])
    done = None
    for t in range(n_sub):
        f = None
        nxt = None
        for c in range(n_chunks):
            d = ffn_chunk(hb, c)
            f = d if f is None else f + d
            if c == 0 and done is not None:
                finish(*done)
            if c == n_chunks // 2 and t + 1 < n_sub:
                nxt = mid_norms(t + 1, ys[t + 1])
        done = (t, x, f)
        if nxt is not None:
            x, hb = nxt
    finish(*done)


def _outproj_ffn(x, yc, ya, layer, wo, gpm, gpf, wi, wd, gqf):
    t, d = x.shape
    tm = TM_FFN
    row = lambda i: (i, 0)
    return pl.pallas_call(
        _ffn_kernel,
        grid=(t // tm,),
        in_specs=[
            pl.BlockSpec((tm, d), row),
            pl.BlockSpec((tm, CONV_WIDTH), row),
            pl.BlockSpec((tm, ATTN_WIDTH), row),
            _layer_spec(wo, layer, 1, single_buffer=True),
            _layer_spec(gpm, layer, 1),
            _layer_spec(gpf, layer, 1),
            _layer_spec(wi, layer, 1, single_buffer=True),
            _layer_spec(wd, layer, 1, single_buffer=True),
            _layer_spec(gqf, layer, 1),
        ],
        out_specs=pl.BlockSpec((tm, d), row),
        out_shape=jax.ShapeDtypeStruct((t, d), F32),
        compiler_params=pltpu.CompilerParams(
            dimension_semantics=("arbitrary",), vmem_limit_bytes=VMEM_LIMIT),
        name="outproj_ffn",
    )(x, yc, ya, wo, gpm, gpf, wi, wd, gqf)


def kernel(x, w_in, w_conv, rel_bias, g_conv_out, g_attn_out, w_out, g_pre_mix, g_post_mix,
           g_pre_ffn, g_post_ffn, w_ffn_in, w_ffn_out):
    b, s, d = x.shape
    depth = w_in.shape[0]
    tabs = _bias_tables(rel_bias)
    w_in_b = w_in.astype(BF16)
    w_out_b = w_out.astype(BF16)
    w_ffn_in_b = w_ffn_in.astype(BF16)
    w_ffn_out_b = w_ffn_out.astype(BF16)
    w_conv_t = jnp.swapaxes(w_conv, 1, 2)
    row3 = lambda a: a.reshape(depth, 1, a.shape[-1])
    g_conv_out, g_attn_out, g_pre_mix, g_post_mix, g_pre_ffn, g_post_ffn = map(
        row3, (g_conv_out, g_attn_out, g_pre_mix, g_post_mix, g_pre_ffn, g_post_ffn))
    for l in range(depth):
        yc, q, k, v = _inproj(x, l, g_pre_mix, w_in_b, w_conv_t, g_conv_out)
        ya = _attention(q, k, v, l, tabs, g_attn_out)
        x = _outproj_ffn(
            x.reshape(b * s, d), yc.reshape(b * s, CONV_WIDTH), ya.reshape(b * s, ATTN_WIDTH), l,
            w_out_b, g_post_mix, g_pre_ffn, w_ffn_in_b, w_ffn_out_b, g_post_ffn).reshape(b, s, d)
    return x
```

```python
import math

import jax
import jax.numpy as jnp
from jax import lax
from jax.experimental import pallas as pl
from jax.experimental.pallas import tpu as pltpu

D_MODEL = 1024
CHUNK = 64
N_LEFT_CHUNKS = 8
LEFT_PAD = N_LEFT_CHUNKS * CHUNK
CONV_WIDTH = 512
ATTN_WIDTH = 512
HEAD_DIM = 64
N_HEADS = 8
CONV_K = 3
REL_CLIP = 128
D_FF = 2816
EPS = 1e-6
NEG_INF = -1e30
LOG2E = math.log2(math.e)

LANES = 128
TM_PROJ = 512
TM_FFN = 1024
SUB_ROWS = 512
TQ = 256
N_KBLK = LEFT_PAD // TQ + 1
NK = N_KBLK * TQ
Q_BLOCKS = 2
TQS = Q_BLOCKS * TQ
SCORE_LEAD = 2
TABLE_WIDTH = 1024
FF_CHUNK = 256
VMEM_LIMIT = 56 * 1024 * 1024

F32 = jnp.float32
BF16 = jnp.bfloat16


def _rms(x, g):
    return (x * lax.rsqrt(jnp.mean(x * x, axis=-1, keepdims=True) + EPS)) * g


def _pair_group_norm(y, g):
    lane = lax.broadcasted_iota(jnp.int32, y.shape, 1)
    lo = lane < HEAD_DIM
    sq = y * y
    ms_lo = jnp.sum(jnp.where(lo, sq, 0.0), axis=-1, keepdims=True) * (1.0 / HEAD_DIM)
    ms_hi = jnp.sum(jnp.where(lo, 0.0, sq), axis=-1, keepdims=True) * (1.0 / HEAD_DIM)
    inv = jnp.where(lo, lax.rsqrt(ms_lo + EPS), lax.rsqrt(ms_hi + EPS))
    return (y * inv) * g


def _layer_spec(a, layer, n_grid, single_buffer=False):
    zeros = (0,) * (a.ndim - 1)
    index = {1: lambda i: (layer,) + zeros, 2: lambda bi, i: (layer,) + zeros}[n_grid]
    mode = pl.Buffered(1) if single_buffer else None
    return pl.BlockSpec((None,) + a.shape[1:], index, pipeline_mode=mode)


def _inproj_kernel(x_ref, g_ref, w_ref, wc_ref, gco_ref, yc_ref, q_ref, k_ref, v_ref, carry_ref):
    @pl.when(pl.program_id(1) == 0)
    def _():
        carry_ref[...] = jnp.zeros_like(carry_ref)

    n_sub = x_ref.shape[1] // SUB_ROWS
    lo = lax.broadcasted_iota(jnp.int32, (SUB_ROWS, LANES), 1) < HEAD_DIM
    row8 = lax.broadcasted_iota(jnp.int32, (8, 1), 0)

    def rows(t):
        return slice(t * SUB_ROWS, (t + 1) * SUB_ROWS)

    def project(hb, cols):
        return [jnp.dot(hb, w_ref[:, j * CONV_WIDTH:(j + 1) * CONV_WIDTH], preferred_element_type=F32)
                for j in cols]

    def conv_branch(t, h, b_gate, c_gate, prev):
        u = c_gate * h
        u1 = pltpu.roll(u, 1, 0)
        u2 = pltpu.roll(u, 2, 0)
        u1 = jnp.concatenate([jnp.where(row8 == 0, prev[7:8], u1[:8]), u1[8:]], axis=0)
        u2 = jnp.concatenate(
            [jnp.where(row8 == 0, prev[6:7], jnp.where(row8 == 1, prev[7:8], u2[:8])), u2[8:]], axis=0)
        conv = b_gate * (u2 * wc_ref[0:1] + u1 * wc_ref[1:2] + u * wc_ref[2:3])
        for j in range(CONV_WIDTH // LANES):
            sl = slice(j * LANES, (j + 1) * LANES)
            yc_ref[0, rows(t), sl] = _pair_group_norm(conv[:, sl], gco_ref[:, sl]).astype(BF16)
        return u[SUB_ROWS - 8:]

    def qkv(t, q, k, v):
        q = q * (LOG2E * HEAD_DIM ** -0.5)
        k_ref[0, rows(t), :] = k.astype(BF16)
        for pair in range(N_HEADS // 2):
            sl = slice(pair * LANES, (pair + 1) * LANES)
            first = slice(2 * pair * LANES, (2 * pair + 1) * LANES)
            second = slice((2 * pair + 1) * LANES, (2 * pair + 2) * LANES)
            q_ref[0, rows(t), first] = jnp.where(lo, q[:, sl], 0.0).astype(BF16)
            q_ref[0, rows(t), second] = jnp.where(lo, 0.0, q[:, sl]).astype(BF16)
            v_ref[0, rows(t), first] = jnp.where(lo, v[:, sl], 1.0).astype(BF16)
            v_ref[0, rows(t), second] = jnp.where(lo, 1.0, v[:, sl]).astype(BF16)

    hbs = [_rms(x_ref[0, rows(t), :], g_ref[...]).astype(BF16) for t in range(n_sub)]
    prev = carry_ref[...]
    proj = project(hbs[0], range(6))
    for t in range(n_sub):
        nxt = project(hbs[t + 1], range(1)) if t + 1 < n_sub else []
        prev = conv_branch(t, *proj[:3], prev)
        if nxt:
            nxt += project(hbs[t + 1], range(1, 3))
        qkv(t, *proj[3:])
        if nxt:
            nxt += project(hbs[t + 1], range(3, 6))
        proj = nxt
    carry_ref[...] = prev


def _inproj(x, layer, g, w, wc, gco):
    b, s, d = x.shape
    tm = TM_PROJ
    row = lambda bi, i: (bi, i, 0)
    narrow = jax.ShapeDtypeStruct((b, s, CONV_WIDTH), BF16)
    wide = jax.ShapeDtypeStruct((b, s, N_HEADS * LANES), BF16)
    narrow_spec = pl.BlockSpec((1, tm, CONV_WIDTH), row)
    wide_spec = pl.BlockSpec((1, tm, N_HEADS * LANES), row)
    return pl.pallas_call(
        _inproj_kernel,
        grid=(b, s // tm),
        in_specs=[
            pl.BlockSpec((1, tm, d), row),
            _layer_spec(g, layer, 2),
            _layer_spec(w, layer, 2, single_buffer=True),
            _layer_spec(wc, layer, 2),
            _layer_spec(gco, layer, 2),
        ],
        out_specs=[narrow_spec, wide_spec, narrow_spec, wide_spec],
        out_shape=[narrow, wide, narrow, wide],
        scratch_shapes=[pltpu.VMEM((8, CONV_WIDTH), F32)],
        compiler_params=pltpu.CompilerParams(
            dimension_semantics=("arbitrary", "arbitrary"), vmem_limit_bytes=VMEM_LIMIT),
        name="inproj_conv",
    )(x, g, w, wc, gco)


def _table_kernel(f_ref, o_ref):
    f = jnp.broadcast_to(f_ref[0], (TQ, TABLE_WIDTH))
    r = pltpu.roll(f, 0, 1, stride=1, stride_axis=0)[:, :NK]
    qc = lax.broadcasted_iota(jnp.int32, (TQ, NK), 0) // CHUNK
    kc = lax.broadcasted_iota(jnp.int32, (TQ, NK), 1) // CHUNK
    valid = (kc >= qc) & (kc <= qc + N_LEFT_CHUNKS)
    o_ref[0] = jnp.where(valid, r * LOG2E, NEG_INF)


def _bias_tables(rel_bias):
    depth, h, _ = rel_bias.shape
    m = jnp.arange(TABLE_WIDTH)
    m = jnp.where(m < NK, m, m - TABLE_WIDTH)
    idx = jnp.clip(LEFT_PAD - m, -REL_CLIP, REL_CLIP) + REL_CLIP
    f = rel_bias.astype(F32)[:, :, idx].reshape(depth * h, 1, TABLE_WIDTH)
    tabs = pl.pallas_call(
        _table_kernel,
        grid=(depth * h,),
        in_specs=[pl.BlockSpec((1, 1, TABLE_WIDTH), lambda i: (i, 0, 0))],
        out_specs=pl.BlockSpec((1, TQ, NK), lambda i: (i, 0, 0)),
        out_shape=jax.ShapeDtypeStruct((depth * h, TQ, NK), F32),
        name="bias_tables",
    )(f)
    return tabs.reshape(depth, h, TQ, NK)


def _attn_kernel(q_ref, kp_ref, kc_ref, vp_ref, vc_ref, tab_ref, g_ref, o_ref):
    i = pl.program_id(1)
    lo = lax.broadcasted_iota(jnp.int32, (TQ, LANES), 1) < HEAD_DIM
    nt = (((1,), (1,)), ((), ()))

    def attend(first_step):
        def key_parts(j):
            lo_row = (j + 1 - N_KBLK) * TQ
            parts = []
            if lo_row < 0 and not first_step:
                parts.append((kp_ref, vp_ref, TQS + lo_row, TQS))
            parts.append((kc_ref, vc_ref, max(lo_row, 0), (j + 1) * TQ))
            return parts

        def scores(unit):
            j, head = unit
            pair = slice(head // 2 * LANES, (head // 2 + 1) * LANES)
            hl = slice(head * LANES, (head + 1) * LANES)
            q = q_ref[0, j * TQ:(j + 1) * TQ, hl]
            s = jnp.concatenate(
                [lax.dot_general(q, k[0, r0:r1, pair], nt, preferred_element_type=F32)
                 for k, _, r0, r1 in key_parts(j)], axis=1)
            s = s + tab_ref[head, :, NK - s.shape[1]:]
            return s, jnp.max(s, axis=-1, keepdims=True)

        def weighted_values(unit, s, mx):
            j, head = unit
            hl = slice(head * LANES, (head + 1) * LANES)
            p = jnp.exp2(s - mx).astype(BF16)
            acc, col = None, 0
            for _, v, r0, r1 in key_parts(j):
                part = jnp.dot(p[:, col:col + r1 - r0], v[0, r0:r1, hl], preferred_element_type=F32)
                acc = part if acc is None else acc + part
                col += r1 - r0
            return acc

        units = [(j, h) for j in range(Q_BLOCKS) for h in range(N_HEADS)]
        pending = [scores(u) for u in units[:SCORE_LEAD]]
        acc = []
        for n, unit in enumerate(units):
            if n + SCORE_LEAD < len(units):
                pending.append(scores(units[n + SCORE_LEAD]))
            acc.append(weighted_values(unit, *pending.pop(0)))
            j, head = unit
            if head % 2 == 1:
                sl = slice(head // 2 * LANES, (head // 2 + 1) * LANES)
                a0, a1 = acc
                acc = []
                y = jnp.where(lo, a0 / pltpu.roll(a0, HEAD_DIM, 1), a1 / pltpu.roll(a1, HEAD_DIM, 1))
                o_ref[0, j * TQ:(j + 1) * TQ, sl] = _pair_group_norm(y, g_ref[:, sl]).astype(BF16)

    @pl.when(i == 0)
    def _():
        attend(True)

    @pl.when(i > 0)
    def _():
        attend(False)


def _attention(q, k, v, layer, tab, g):
    b, s, w = k.shape
    cur = lambda bi, i: (bi, i, 0)
    prev = lambda bi, i: (bi, jnp.maximum(i - 1, 0), 0)
    return pl.pallas_call(
        _attn_kernel,
        grid=(b, s // TQS),
        in_specs=[
            pl.BlockSpec((1, TQS, q.shape[-1]), cur),
            pl.BlockSpec((1, TQS, w), prev),
            pl.BlockSpec((1, TQS, w), cur),
            pl.BlockSpec((1, TQS, v.shape[-1]), prev),
            pl.BlockSpec((1, TQS, v.shape[-1]), cur),
            _layer_spec(tab, layer, 2, single_buffer=True),
            _layer_spec(g, layer, 2),
        ],
        out_specs=pl.BlockSpec((1, TQS, w), cur),
        out_shape=jax.ShapeDtypeStruct((b, s, w), BF16),
        compiler_params=pltpu.CompilerParams(
            dimension_semantics=("arbitrary", "arbitrary"), vmem_limit_bytes=VMEM_LIMIT),
        name="band_attention",
    )(q, k, k, v, v, tab, g)


def _ffn_kernel(x_ref, yc_ref, ya_ref, wo_ref, gpm_ref, gpf_ref, wi_ref, wd_ref, gqf_ref, o_ref):
    n_sub = x_ref.shape[0] // SUB_ROWS
    n_chunks = D_FF // FF_CHUNK

    def rows(t):
        return slice(t * SUB_ROWS, (t + 1) * SUB_ROWS)

    def out_proj(t):
        y = jnp.dot(yc_ref[rows(t), :], wo_ref[0:CONV_WIDTH], preferred_element_type=F32)
        return y + jnp.dot(ya_ref[rows(t), :], wo_ref[CONV_WIDTH:], preferred_element_type=F32)

    def mid_norms(t, y):
        x = x_ref[rows(t), :] + _rms(y, gpm_ref[...])
        return x, _rms(x, gpf_ref[...]).astype(BF16)

    def ffn_chunk(hb, c):
        sl = slice(c * FF_CHUNK, (c + 1) * FF_CHUNK)
        sl_up = slice(D_FF + c * FF_CHUNK, D_FF + (c + 1) * FF_CHUNK)
        gate = jnp.dot(hb, wi_ref[:, sl], preferred_element_type=F32)
        up = jnp.dot(hb, wi_ref[:, sl_up], preferred_element_type=F32)
        a = ((gate * jax.nn.sigmoid(gate)) * up).astype(BF16)
        return jnp.dot(a, wd_ref[sl, :], preferred_element_type=F32)

    def finish(t, x, f):
        o_ref[rows(t), :] = x + _rms(f, gqf_ref[...])

    ys = [out_proj(t) for t in range(n_sub)]
    x, hb = mid_norms(0, ys[0])
    done = None
    for t in range(n_sub):
        f = None
        nxt = None
        for c in range(n_chunks):
            d = ffn_chunk(hb, c)
            f = d if f is None else f + d
            if c == 0 and done is not None:
                finish(*done)
            if c == n_chunks // 2 and t + 1 < n_sub:
                nxt = mid_norms(t + 1, ys[t + 1])
        done = (t, x, f)
        if nxt is not None:
            x, hb = nxt
    finish(*done)


def _outproj_ffn(x, yc, ya, layer, wo, gpm, gpf, wi, wd, gqf):
    t, d = x.shape
    tm = TM_FFN
    row = lambda i: (i, 0)
    return pl.pallas_call(
        _ffn_kernel,
        grid=(t // tm,),
        in_specs=[
            pl.BlockSpec((tm, d), row),
            pl.BlockSpec((tm, CONV_WIDTH), row),
            pl.BlockSpec((tm, ATTN_WIDTH), row),
            _layer_spec(wo, layer, 1, single_buffer=True),
            _layer_spec(gpm, layer, 1),
            _layer_spec(gpf, layer, 1),
            _layer_spec(wi, layer, 1, single_buffer=True),
            _layer_spec(wd, layer, 1, single_buffer=True),
            _layer_spec(gqf, layer, 1),
        ],
        out_specs=pl.BlockSpec((tm, d), row),
        out_shape=jax.ShapeDtypeStruct((t, d), F32),
        compiler_params=pltpu.CompilerParams(
            dimension_semantics=("arbitrary",), vmem_limit_bytes=VMEM_LIMIT),
        name="outproj_ffn",
    )(x, yc, ya, wo, gpm, gpf, wi, wd, gqf)


def kernel(x, w_in, w_conv, rel_bias, g_conv_out, g_attn_out, w_out, g_pre_mix, g_post_mix,
           g_pre_ffn, g_post_ffn, w_ffn_in, w_ffn_out):
    b, s, d = x.shape
    depth = w_in.shape[0]
    tabs = _bias_tables(rel_bias)
    w_in_b = w_in.astype(BF16)
    w_out_b = w_out.astype(BF16)
    w_ffn_in_b = w_ffn_in.astype(BF16)
    w_ffn_out_b = w_ffn_out.astype(BF16)
    w_conv_t = jnp.swapaxes(w_conv, 1, 2)
    row3 = lambda a: a.reshape(depth, 1, a.shape[-1])
    g_conv_out, g_attn_out, g_pre_mix, g_post_mix, g_pre_ffn, g_post_ffn = map(
        row3, (g_conv_out, g_attn_out, g_pre_mix, g_post_mix, g_pre_ffn, g_post_ffn))
    for l in range(depth):
        yc, q, k, v = _inproj(x, l, g_pre_mix, w_in_b, w_conv_t, g_conv_out)
        ya = _attention(q, k, v, l, tabs, g_attn_out)
        x = _outproj_ffn(
            x.reshape(b * s, d), yc.reshape(b * s, CONV_WIDTH), ya.reshape(b * s, ATTN_WIDTH), l,
            w_out_b, g_post_mix, g_pre_ffn, w_ffn_in_b, w_ffn_out_b, g_post_ffn).reshape(b, s, d)
    return x
```

```python
import math

import jax
import jax.numpy as jnp
from jax import lax
from jax.experimental import pallas as pl
from jax.experimental.pallas import tpu as pltpu

D_MODEL = 1024
CHUNK = 64
N_LEFT_CHUNKS = 8
LEFT_PAD = N_LEFT_CHUNKS * CHUNK
CONV_WIDTH = 512
ATTN_WIDTH = 512
HEAD_DIM = 64
N_HEADS = 8
CONV_K = 3
REL_CLIP = 128
D_FF = 2816
EPS = 1e-6
NEG_INF = -1e30
LOG2E = math.log2(math.e)

LANES = 128
TM_PROJ = 512
TM_FFN = 1024
SUB_ROWS = 512
TQ = 256
N_KBLK = LEFT_PAD // TQ + 1
NK = N_KBLK * TQ
Q_BLOCKS = 2
TQS = Q_BLOCKS * TQ
SCORE_LEAD = 2
FFN_OUT_CAST_STEPS = 2
TABLE_WIDTH = 1024
FF_CHUNK = 256
VMEM_LIMIT = 56 * 1024 * 1024

F32 = jnp.float32
BF16 = jnp.bfloat16


def _rms(x, g):
    return (x * lax.rsqrt(jnp.mean(x * x, axis=-1, keepdims=True) + EPS)) * g


def _pair_group_norm(y, g):
    lane = lax.broadcasted_iota(jnp.int32, y.shape, 1)
    lo = lane < HEAD_DIM
    sq = y * y
    ms_lo = jnp.sum(jnp.where(lo, sq, 0.0), axis=-1, keepdims=True) * (1.0 / HEAD_DIM)
    ms_hi = jnp.sum(jnp.where(lo, 0.0, sq), axis=-1, keepdims=True) * (1.0 / HEAD_DIM)
    inv = jnp.where(lo, lax.rsqrt(ms_lo + EPS), lax.rsqrt(ms_hi + EPS))
    return (y * inv) * g


def _layer_spec(a, layer, n_grid, single_buffer=False):
    zeros = (0,) * (a.ndim - 1)
    index = {1: lambda i: (layer,) + zeros, 2: lambda bi, i: (layer,) + zeros}[n_grid]
    mode = pl.Buffered(1) if single_buffer else None
    return pl.BlockSpec((None,) + a.shape[1:], index, pipeline_mode=mode)


def _whole_spec(a, n_grid):
    index = {1: lambda i: (0, 0), 2: lambda bi, i: (0, 0)}[n_grid]
    return pl.BlockSpec(a.shape, index, pipeline_mode=pl.Buffered(1))


def _inproj_kernel(x_ref, g_ref, w_ref, wc_ref, gco_ref, yc_ref, q_ref, k_ref, v_ref, carry_ref):
    @pl.when(pl.program_id(1) == 0)
    def _():
        carry_ref[...] = jnp.zeros_like(carry_ref)

    tm = x_ref.shape[1]
    lo = lax.broadcasted_iota(jnp.int32, (tm, LANES), 1) < HEAD_DIM
    row8 = lax.broadcasted_iota(jnp.int32, (8, 1), 0)
    hb = _rms(x_ref[0], g_ref[...]).astype(BF16)

    def project(j):
        return jnp.dot(hb, w_ref[:, j * CONV_WIDTH:(j + 1) * CONV_WIDTH], preferred_element_type=F32)

    h = project(0)
    u = project(2) * h
    prev = carry_ref[...]
    u1 = pltpu.roll(u, 1, 0)
    u2 = pltpu.roll(u, 2, 0)
    u1 = jnp.concatenate([jnp.where(row8 == 0, prev[7:8], u1[:8]), u1[8:]], axis=0)
    u2 = jnp.concatenate(
        [jnp.where(row8 == 0, prev[6:7], jnp.where(row8 == 1, prev[7:8], u2[:8])), u2[8:]], axis=0)
    carry_ref[...] = u[tm - 8:]
    conv = project(1) * (u2 * wc_ref[0:1] + u1 * wc_ref[1:2] + u * wc_ref[2:3])
    for j in range(CONV_WIDTH // LANES):
        sl = slice(j * LANES, (j + 1) * LANES)
        yc_ref[0, :, sl] = _pair_group_norm(conv[:, sl], gco_ref[:, sl]).astype(BF16)

    def head_groups(ref, val, fill):
        for pair in range(N_HEADS // 2):
            sl = slice(pair * LANES, (pair + 1) * LANES)
            first = slice(2 * pair * LANES, (2 * pair + 1) * LANES)
            second = slice((2 * pair + 1) * LANES, (2 * pair + 2) * LANES)
            ref[0, :, first] = jnp.where(lo, val[:, sl], fill).astype(BF16)
            ref[0, :, second] = jnp.where(lo, fill, val[:, sl]).astype(BF16)

    head_groups(q_ref, project(3) * (LOG2E * HEAD_DIM ** -0.5), 0.0)
    k_ref[0] = project(4).astype(BF16)
    head_groups(v_ref, project(5), 1.0)


def _inproj(x, layer, g, w, wc, gco):
    b, s, d = x.shape
    tm = TM_PROJ
    row = lambda bi, i: (bi, i, 0)
    narrow = jax.ShapeDtypeStruct((b, s, CONV_WIDTH), BF16)
    wide = jax.ShapeDtypeStruct((b, s, N_HEADS * LANES), BF16)
    narrow_spec = pl.BlockSpec((1, tm, CONV_WIDTH), row)
    wide_spec = pl.BlockSpec((1, tm, N_HEADS * LANES), row)
    return pl.pallas_call(
        _inproj_kernel,
        grid=(b, s // tm),
        in_specs=[
            pl.BlockSpec((1, tm, d), row),
            _layer_spec(g, layer, 2),
            _whole_spec(w, 2),
            _layer_spec(wc, layer, 2),
            _layer_spec(gco, layer, 2),
        ],
        out_specs=[narrow_spec, wide_spec, narrow_spec, wide_spec],
        out_shape=[narrow, wide, narrow, wide],
        scratch_shapes=[pltpu.VMEM((8, CONV_WIDTH), F32)],
        compiler_params=pltpu.CompilerParams(
            dimension_semantics=("arbitrary", "arbitrary"), vmem_limit_bytes=VMEM_LIMIT),
        name="inproj_conv",
    )(x, g, w, wc, gco)


def _table_kernel(f_ref, o_ref):
    qc = lax.broadcasted_iota(jnp.int32, (TQ, NK), 0) // CHUNK
    kc = lax.broadcasted_iota(jnp.int32, (TQ, NK), 1) // CHUNK
    valid = (kc >= qc) & (kc <= qc + N_LEFT_CHUNKS)
    for head in range(N_HEADS):
        f = jnp.broadcast_to(f_ref[head], (TQ, TABLE_WIDTH))
        r = pltpu.roll(f, 0, 1, stride=1, stride_axis=0)[:, :NK]
        o_ref[head] = jnp.where(valid, r * LOG2E, NEG_INF)


def _bias_tables(rel_bias):
    depth, h, _ = rel_bias.shape
    m = jnp.arange(TABLE_WIDTH)
    m = jnp.where(m < NK, m, m - TABLE_WIDTH)
    idx = jnp.clip(LEFT_PAD - m, -REL_CLIP, REL_CLIP) + REL_CLIP
    f = rel_bias.astype(F32)[:, :, idx].reshape(depth, h, 1, TABLE_WIDTH)
    return pl.pallas_call(
        _table_kernel,
        grid=(depth,),
        in_specs=[pl.BlockSpec((None, h, 1, TABLE_WIDTH), lambda i: (i, 0, 0, 0))],
        out_specs=pl.BlockSpec((None, h, TQ, NK), lambda i: (i, 0, 0, 0)),
        out_shape=jax.ShapeDtypeStruct((depth, h, TQ, NK), F32),
        name="bias_tables",
    )(f)


def _attn_kernel(q_ref, kp_ref, kc_ref, vp_ref, vc_ref, tab_ref, g_ref, *rest):
    n_cast = len(rest) // 2
    o_ref = rest[n_cast]
    for src, dst in zip(rest[:n_cast], rest[n_cast + 1:]):
        dst[...] = src[...].astype(BF16)

    i = pl.program_id(1)
    lo = lax.broadcasted_iota(jnp.int32, (TQ, LANES), 1) < HEAD_DIM
    nt = (((1,), (1,)), ((), ()))

    def attend(first_step):
        def key_parts(j):
            lo_row = (j + 1 - N_KBLK) * TQ
            parts = []
            if lo_row < 0 and not first_step:
                parts.append((kp_ref, vp_ref, TQS + lo_row, TQS))
            parts.append((kc_ref, vc_ref, max(lo_row, 0), (j + 1) * TQ))
            return parts

        def scores(unit):
            j, head = unit
            pair = slice(head // 2 * LANES, (head // 2 + 1) * LANES)
            hl = slice(head * LANES, (head + 1) * LANES)
            q = q_ref[0, j * TQ:(j + 1) * TQ, hl]
            s = jnp.concatenate(
                [lax.dot_general(q, k[0, r0:r1, pair], nt, preferred_element_type=F32)
                 for k, _, r0, r1 in key_parts(j)], axis=1)
            s = s + tab_ref[head, :, NK - s.shape[1]:]
            return s, jnp.max(s, axis=-1, keepdims=True)

        def weighted_values(unit, s, mx):
            j, head = unit
            hl = slice(head * LANES, (head + 1) * LANES)
            p = jnp.exp2(s - mx).astype(BF16)
            acc, col = None, 0
            for _, v, r0, r1 in key_parts(j):
                part = jnp.dot(p[:, col:col + r1 - r0], v[0, r0:r1, hl], preferred_element_type=F32)
                acc = part if acc is None else acc + part
                col += r1 - r0
            return acc

        units = [(j, h) for j in range(Q_BLOCKS) for h in range(N_HEADS)]
        pending = [scores(u) for u in units[:SCORE_LEAD]]
        acc = []
        for n, unit in enumerate(units):
            if n + SCORE_LEAD < len(units):
                pending.append(scores(units[n + SCORE_LEAD]))
            acc.append(weighted_values(unit, *pending.pop(0)))
            j, head = unit
            if head % 2 == 1:
                sl = slice(head // 2 * LANES, (head // 2 + 1) * LANES)
                a0, a1 = acc
                acc = []
                y = jnp.where(lo, a0 / pltpu.roll(a0, HEAD_DIM, 1), a1 / pltpu.roll(a1, HEAD_DIM, 1))
                o_ref[0, j * TQ:(j + 1) * TQ, sl] = _pair_group_norm(y, g_ref[:, sl]).astype(BF16)

    @pl.when(i == 0)
    def _():
        attend(True)

    @pl.when(i > 0)
    def _():
        attend(False)


def _attention(q, k, v, layer, tab, g, casts):
    b, s, w = k.shape
    n_i = s // TQS
    n_steps = b * n_i
    cur = lambda bi, i: (bi, i, 0)
    prev = lambda bi, i: (bi, jnp.maximum(i - 1, 0), 0)
    cast_in, cast_out, cast_shape = [], [], []
    for a, lyr, per in casts:
        _, r, c = a.shape
        rows = r * per // n_steps
        cast_in.append(pl.BlockSpec(
            (None, rows, c), lambda bi, i, lyr=lyr, per=per: (lyr, (bi * n_i + i) // per, 0)))
        cast_out.append(pl.BlockSpec((rows, c), lambda bi, i, per=per: ((bi * n_i + i) // per, 0)))
        cast_shape.append(jax.ShapeDtypeStruct((r, c), BF16))
    out = pl.pallas_call(
        _attn_kernel,
        grid=(b, n_i),
        in_specs=[
            pl.BlockSpec((1, TQS, q.shape[-1]), cur),
            pl.BlockSpec((1, TQS, w), prev),
            pl.BlockSpec((1, TQS, w), cur),
            pl.BlockSpec((1, TQS, v.shape[-1]), prev),
            pl.BlockSpec((1, TQS, v.shape[-1]), cur),
            _layer_spec(tab, layer, 2, single_buffer=True),
            _layer_spec(g, layer, 2),
        ] + cast_in,
        out_specs=[pl.BlockSpec((1, TQS, w), cur)] + cast_out,
        out_shape=[jax.ShapeDtypeStruct((b, s, w), BF16)] + cast_shape,
        compiler_params=pltpu.CompilerParams(
            dimension_semantics=("arbitrary", "arbitrary"), vmem_limit_bytes=VMEM_LIMIT),
        name="band_attention",
    )(q, k, k, v, v, tab, g, *[a for a, _, _ in casts])
    return out[0], out[1:]


def _ffn_kernel(x_ref, yc_ref, ya_ref, wo_ref, gpm_ref, gpf_ref, wi_ref, wd_ref, gqf_ref, o_ref):
    n_sub = x_ref.shape[0] // SUB_ROWS
    n_chunks = D_FF // FF_CHUNK

    def rows(t):
        return slice(t * SUB_ROWS, (t + 1) * SUB_ROWS)

    def out_proj(t):
        y = jnp.dot(yc_ref[rows(t), :], wo_ref[0:CONV_WIDTH], preferred_element_type=F32)
        return y + jnp.dot(ya_ref[rows(t), :], wo_ref[CONV_WIDTH:], preferred_element_type=F32)

    def mid_norms(t, y):
        x = x_ref[rows(t), :] + _rms(y, gpm_ref[...])
        return x, _rms(x, gpf_ref[...]).astype(BF16)

    def ffn_chunk(hb, c):
        sl = slice(c * FF_CHUNK, (c + 1) * FF_CHUNK)
        sl_up = slice(D_FF + c * FF_CHUNK, D_FF + (c + 1) * FF_CHUNK)
        gate = jnp.dot(hb, wi_ref[:, sl], preferred_element_type=F32)
        up = jnp.dot(hb, wi_ref[:, sl_up], preferred_element_type=F32)
        a = ((gate * jax.nn.sigmoid(gate)) * up).astype(BF16)
        return jnp.dot(a, wd_ref[sl, :], preferred_element_type=F32)

    def finish(t, x, f):
        o_ref[rows(t), :] = x + _rms(f, gqf_ref[...])

    ys = [out_proj(t) for t in range(n_sub)]
    x, hb = mid_norms(0, ys[0])
    done = None
    for t in range(n_sub):
        f = None
        nxt = None
        for c in range(n_chunks):
            d = ffn_chunk(hb, c)
            f = d if f is None else f + d
            if c == 0 and done is not None:
                finish(*done)
            if c == n_chunks // 2 and t + 1 < n_sub:
                nxt = mid_norms(t + 1, ys[t + 1])
        done = (t, x, f)
        if nxt is not None:
            x, hb = nxt
    finish(*done)


def _outproj_ffn(x, yc, ya, layer, wo, gpm, gpf, wi, wd, gqf):
    t, d = x.shape
    tm = TM_FFN
    row = lambda i: (i, 0)
    return pl.pallas_call(
        _ffn_kernel,
        grid=(t // tm,),
        in_specs=[
            pl.BlockSpec((tm, d), row),
            pl.BlockSpec((tm, CONV_WIDTH), row),
            pl.BlockSpec((tm, ATTN_WIDTH), row),
            _whole_spec(wo, 1),
            _layer_spec(gpm, layer, 1),
            _layer_spec(gpf, layer, 1),
            _whole_spec(wi, 1),
            _whole_spec(wd, 1),
            _layer_spec(gqf, layer, 1),
        ],
        out_specs=pl.BlockSpec((tm, d), row),
        out_shape=jax.ShapeDtypeStruct((t, d), F32),
        compiler_params=pltpu.CompilerParams(
            dimension_semantics=("arbitrary",), vmem_limit_bytes=VMEM_LIMIT),
        name="outproj_ffn",
    )(x, yc, ya, wo, gpm, gpf, wi, wd, gqf)


def kernel(x, w_in, w_conv, rel_bias, g_conv_out, g_attn_out, w_out, g_pre_mix, g_post_mix,
           g_pre_ffn, g_post_ffn, w_ffn_in, w_ffn_out):
    b, s, d = x.shape
    depth = w_in.shape[0]
    tabs = _bias_tables(rel_bias)
    w_conv_t = jnp.swapaxes(w_conv, 1, 2)
    row3 = lambda a: a.reshape(depth, 1, a.shape[-1])
    g_conv_out, g_attn_out, g_pre_mix, g_post_mix, g_pre_ffn, g_post_ffn = map(
        row3, (g_conv_out, g_attn_out, g_pre_mix, g_post_mix, g_pre_ffn, g_post_ffn))
    w_in_b = w_in[0].astype(BF16)
    for l in range(depth):
        yc, q, k, v = _inproj(x, l, g_pre_mix, w_in_b, w_conv_t, g_conv_out)
        casts = [(w_out, l, 1), (w_ffn_in, l, 1), (w_ffn_out, l, FFN_OUT_CAST_STEPS)]
        if l + 1 < depth:
            casts.append((w_in, l + 1, 1))
        ya, cast = _attention(q, k, v, l, tabs, g_attn_out, casts)
        x = _outproj_ffn(
            x.reshape(b * s, d), yc.reshape(b * s, CONV_WIDTH), ya.reshape(b * s, ATTN_WIDTH), l,
            cast[0], g_post_mix, g_pre_ffn, cast[1], cast[2], g_post_ffn).reshape(b, s, d)
        if l + 1 < depth:
            w_in_b = cast[3]
    return x
```

```python
import math

import jax
import jax.numpy as jnp
from jax import lax
from jax.experimental import pallas as pl
from jax.experimental.pallas import tpu as pltpu

D_MODEL = 1024
CHUNK = 64
N_LEFT_CHUNKS = 8
LEFT_PAD = N_LEFT_CHUNKS * CHUNK
CONV_WIDTH = 512
ATTN_WIDTH = 512
HEAD_DIM = 64
N_HEADS = 8
CONV_K = 3
REL_CLIP = 128
D_FF = 2816
EPS = 1e-6
NEG_INF = -1e30
LOG2E = math.log2(math.e)

LANES = 128
TM_PROJ = 1024
TM_FFN = 1024
SUB_ROWS = 512
TQ = 256
N_KBLK = LEFT_PAD // TQ + 1
NK = N_KBLK * TQ
Q_BLOCKS = 4
TQS = Q_BLOCKS * TQ
SCORE_LEAD = 2
BF16_ROW_TILE = 16
TABLE_WIDTH = 1024
FF_CHUNK = 256
VMEM_LIMIT = 56 * 1024 * 1024

F32 = jnp.float32
BF16 = jnp.bfloat16


def _rms(x, g):
    return (x * lax.rsqrt(jnp.mean(x * x, axis=-1, keepdims=True) + EPS)) * g


def _pair_group_norm(y, g):
    lane = lax.broadcasted_iota(jnp.int32, y.shape, 1)
    lo = lane < HEAD_DIM
    sq = y * y
    ms_lo = jnp.sum(jnp.where(lo, sq, 0.0), axis=-1, keepdims=True) * (1.0 / HEAD_DIM)
    ms_hi = jnp.sum(jnp.where(lo, 0.0, sq), axis=-1, keepdims=True) * (1.0 / HEAD_DIM)
    inv = jnp.where(lo, lax.rsqrt(ms_lo + EPS), lax.rsqrt(ms_hi + EPS))
    return (y * inv) * g


def _layer_spec(a, layer, n_grid, single_buffer=False):
    zeros = (0,) * (a.ndim - 1)
    index = {1: lambda i: (layer,) + zeros, 2: lambda bi, i: (layer,) + zeros}[n_grid]
    mode = pl.Buffered(1) if single_buffer else None
    return pl.BlockSpec((None,) + a.shape[1:], index, pipeline_mode=mode)


def _whole_spec(a, n_grid):
    index = {1: lambda i: (0, 0), 2: lambda bi, i: (0, 0)}[n_grid]
    return pl.BlockSpec(a.shape, index, pipeline_mode=pl.Buffered(1))


def _inproj_kernel(x_ref, g_ref, w_ref, wc_ref, gco_ref, yc_ref, q_ref, k_ref, v_ref, carry_ref):
    @pl.when(pl.program_id(1) == 0)
    def _():
        carry_ref[...] = jnp.zeros_like(carry_ref)

    n_sub = x_ref.shape[1] // SUB_ROWS
    lo = lax.broadcasted_iota(jnp.int32, (SUB_ROWS, LANES), 1) < HEAD_DIM
    row8 = lax.broadcasted_iota(jnp.int32, (8, 1), 0)

    def sub_tile(t, hb, prev):
        rows = slice(t * SUB_ROWS, (t + 1) * SUB_ROWS)

        def project(j):
            return jnp.dot(hb, w_ref[:, j * CONV_WIDTH:(j + 1) * CONV_WIDTH], preferred_element_type=F32)

        h = project(0)
        u = project(2) * h
        u1 = pltpu.roll(u, 1, 0)
        u2 = pltpu.roll(u, 2, 0)
        u1 = jnp.concatenate([jnp.where(row8 == 0, prev[7:8], u1[:8]), u1[8:]], axis=0)
        u2 = jnp.concatenate(
            [jnp.where(row8 == 0, prev[6:7], jnp.where(row8 == 1, prev[7:8], u2[:8])), u2[8:]], axis=0)
        conv = project(1) * (u2 * wc_ref[0:1] + u1 * wc_ref[1:2] + u * wc_ref[2:3])
        for j in range(CONV_WIDTH // LANES):
            sl = slice(j * LANES, (j + 1) * LANES)
            yc_ref[0, rows, sl] = _pair_group_norm(conv[:, sl], gco_ref[:, sl]).astype(BF16)

        def head_groups(ref, val, fill):
            for pair in range(N_HEADS // 2):
                sl = slice(pair * LANES, (pair + 1) * LANES)
                first = slice(2 * pair * LANES, (2 * pair + 1) * LANES)
                second = slice((2 * pair + 1) * LANES, (2 * pair + 2) * LANES)
                ref[0, rows, first] = jnp.where(lo, val[:, sl], fill).astype(BF16)
                ref[0, rows, second] = jnp.where(lo, fill, val[:, sl]).astype(BF16)

        head_groups(q_ref, project(3) * (LOG2E * HEAD_DIM ** -0.5), 0.0)
        k_ref[0, rows, :] = project(4).astype(BF16)
        head_groups(v_ref, project(5), 1.0)
        return u[SUB_ROWS - 8:]

    prev = carry_ref[...]
    for t in range(n_sub):
        hb = _rms(x_ref[0, t * SUB_ROWS:(t + 1) * SUB_ROWS, :], g_ref[...]).astype(BF16)
        prev = sub_tile(t, hb, prev)
    carry_ref[...] = prev


def _inproj(x, layer, g, w, wc, gco):
    b, s, d = x.shape
    tm = TM_PROJ
    row = lambda bi, i: (bi, i, 0)
    narrow = jax.ShapeDtypeStruct((b, s, CONV_WIDTH), BF16)
    wide = jax.ShapeDtypeStruct((b, s, N_HEADS * LANES), BF16)
    narrow_spec = pl.BlockSpec((1, tm, CONV_WIDTH), row)
    wide_spec = pl.BlockSpec((1, tm, N_HEADS * LANES), row)
    return pl.pallas_call(
        _inproj_kernel,
        grid=(b, s // tm),
        in_specs=[
            pl.BlockSpec((1, tm, d), row),
            _layer_spec(g, layer, 2),
            _whole_spec(w, 2),
            _layer_spec(wc, layer, 2),
            _layer_spec(gco, layer, 2),
        ],
        out_specs=[narrow_spec, wide_spec, narrow_spec, wide_spec],
        out_shape=[narrow, wide, narrow, wide],
        scratch_shapes=[pltpu.VMEM((8, CONV_WIDTH), F32)],
        compiler_params=pltpu.CompilerParams(
            dimension_semantics=("arbitrary", "arbitrary"), vmem_limit_bytes=VMEM_LIMIT),
        name="inproj_conv",
    )(x, g, w, wc, gco)


def _table_kernel(f_ref, o_ref):
    qc = lax.broadcasted_iota(jnp.int32, (TQ, NK), 0) // CHUNK
    kc = lax.broadcasted_iota(jnp.int32, (TQ, NK), 1) // CHUNK
    valid = (kc >= qc) & (kc <= qc + N_LEFT_CHUNKS)
    for head in range(N_HEADS):
        f = jnp.broadcast_to(f_ref[head], (TQ, TABLE_WIDTH))
        r = pltpu.roll(f, 0, 1, stride=1, stride_axis=0)[:, :NK]
        o_ref[head] = jnp.where(valid, r * LOG2E, NEG_INF)


def _bias_tables(rel_bias):
    depth, h, _ = rel_bias.shape
    m = jnp.arange(TABLE_WIDTH)
    m = jnp.where(m < NK, m, m - TABLE_WIDTH)
    idx = jnp.clip(LEFT_PAD - m, -REL_CLIP, REL_CLIP) + REL_CLIP
    f = rel_bias.astype(F32)[:, :, idx].reshape(depth, h, 1, TABLE_WIDTH)
    return pl.pallas_call(
        _table_kernel,
        grid=(depth,),
        in_specs=[pl.BlockSpec((None, h, 1, TABLE_WIDTH), lambda i: (i, 0, 0, 0))],
        out_specs=pl.BlockSpec((None, h, TQ, NK), lambda i: (i, 0, 0, 0)),
        out_shape=jax.ShapeDtypeStruct((depth, h, TQ, NK), F32),
        name="bias_tables",
    )(f)


def _attn_kernel(q_ref, kp_ref, kc_ref, vp_ref, vc_ref, tab_ref, g_ref, *rest):
    n_cast = len(rest) // 2
    o_ref = rest[n_cast]
    for src, dst in zip(rest[:n_cast], rest[n_cast + 1:]):
        dst[...] = src[...].astype(BF16)

    i = pl.program_id(1)
    lo = lax.broadcasted_iota(jnp.int32, (TQ, LANES), 1) < HEAD_DIM
    nt = (((1,), (1,)), ((), ()))

    def attend(first_step):
        def key_parts(j):
            lo_row = (j + 1 - N_KBLK) * TQ
            parts = []
            if lo_row < 0 and not first_step:
                parts.append((kp_ref, vp_ref, LEFT_PAD + lo_row, LEFT_PAD))
            parts.append((kc_ref, vc_ref, max(lo_row, 0), (j + 1) * TQ))
            return parts

        def scores(unit):
            j, head = unit
            pair = slice(head // 2 * LANES, (head // 2 + 1) * LANES)
            hl = slice(head * LANES, (head + 1) * LANES)
            q = q_ref[0, j * TQ:(j + 1) * TQ, hl]
            s = jnp.concatenate(
                [lax.dot_general(q, k[0, r0:r1, pair], nt, preferred_element_type=F32)
                 for k, _, r0, r1 in key_parts(j)], axis=1)
            s = s + tab_ref[head, :, NK - s.shape[1]:]
            return s, jnp.max(s, axis=-1, keepdims=True)

        def weighted_values(unit, s, mx):
            j, head = unit
            hl = slice(head * LANES, (head + 1) * LANES)
            p = jnp.exp2(s - mx).astype(BF16)
            acc, col = None, 0
            for _, v, r0, r1 in key_parts(j):
                part = jnp.dot(p[:, col:col + r1 - r0], v[0, r0:r1, hl], preferred_element_type=F32)
                acc = part if acc is None else acc + part
                col += r1 - r0
            return acc

        units = [(j, h) for j in range(Q_BLOCKS) for h in range(N_HEADS)]
        pending = [scores(u) for u in units[:SCORE_LEAD]]
        acc = []
        for n, unit in enumerate(units):
            if n + SCORE_LEAD < len(units):
                pending.append(scores(units[n + SCORE_LEAD]))
            acc.append(weighted_values(unit, *pending.pop(0)))
            j, head = unit
            if head % 2 == 1:
                sl = slice(head // 2 * LANES, (head // 2 + 1) * LANES)
                a0, a1 = acc
                acc = []
                y = jnp.where(lo, a0 / pltpu.roll(a0, HEAD_DIM, 1), a1 / pltpu.roll(a1, HEAD_DIM, 1))
                o_ref[0, j * TQ:(j + 1) * TQ, sl] = _pair_group_norm(y, g_ref[:, sl]).astype(BF16)

    @pl.when(i == 0)
    def _():
        attend(True)

    @pl.when(i > 0)
    def _():
        attend(False)


def _attention(q, k, v, layer, tab, g, casts):
    b, s, w = k.shape
    n_i = s // TQS
    n_steps = b * n_i
    cur = lambda bi, i: (bi, i, 0)
    prev = lambda bi, i: (bi, jnp.maximum(i * (TQS // LEFT_PAD) - 1, 0), 0)
    cast_in, cast_out, cast_shape = [], [], []
    for a, lyr in casts:
        _, r, c = a.shape
        per = 1
        while r * per % (n_steps * BF16_ROW_TILE):
            per *= 2
        rows = r * per // n_steps
        cast_in.append(pl.BlockSpec(
            (None, rows, c), lambda bi, i, lyr=lyr, per=per: (lyr, (bi * n_i + i) // per, 0)))
        cast_out.append(pl.BlockSpec((rows, c), lambda bi, i, per=per: ((bi * n_i + i) // per, 0)))
        cast_shape.append(jax.ShapeDtypeStruct((r, c), BF16))
    out = pl.pallas_call(
        _attn_kernel,
        grid=(b, n_i),
        in_specs=[
            pl.BlockSpec((1, TQS, q.shape[-1]), cur),
            pl.BlockSpec((1, LEFT_PAD, w), prev),
            pl.BlockSpec((1, TQS, w), cur),
            pl.BlockSpec((1, LEFT_PAD, v.shape[-1]), prev),
            pl.BlockSpec((1, TQS, v.shape[-1]), cur),
            _layer_spec(tab, layer, 2, single_buffer=True),
            _layer_spec(g, layer, 2),
        ] + cast_in,
        out_specs=[pl.BlockSpec((1, TQS, w), cur)] + cast_out,
        out_shape=[jax.ShapeDtypeStruct((b, s, w), BF16)] + cast_shape,
        compiler_params=pltpu.CompilerParams(
            dimension_semantics=("arbitrary", "arbitrary"), vmem_limit_bytes=VMEM_LIMIT),
        name="band_attention",
    )(q, k, k, v, v, tab, g, *[a for a, _ in casts])
    return out[0], out[1:]


def _ffn_kernel(x_ref, yc_ref, ya_ref, wo_ref, gpm_ref, gpf_ref, wi_ref, wd_ref, gqf_ref, o_ref):
    n_sub = x_ref.shape[0] // SUB_ROWS
    n_chunks = D_FF // FF_CHUNK

    def rows(t):
        return slice(t * SUB_ROWS, (t + 1) * SUB_ROWS)

    def out_proj(t):
        y = jnp.dot(yc_ref[rows(t), :], wo_ref[0:CONV_WIDTH], preferred_element_type=F32)
        return y + jnp.dot(ya_ref[rows(t), :], wo_ref[CONV_WIDTH:], preferred_element_type=F32)

    def mid_norms(t, y):
        x = x_ref[rows(t), :] + _rms(y, gpm_ref[...])
        return x, _rms(x, gpf_ref[...]).astype(BF16)

    def ffn_chunk(hb, c):
        sl = slice(c * FF_CHUNK, (c + 1) * FF_CHUNK)
        sl_up = slice(D_FF + c * FF_CHUNK, D_FF + (c + 1) * FF_CHUNK)
        gate = jnp.dot(hb, wi_ref[:, sl], preferred_element_type=F32)
        up = jnp.dot(hb, wi_ref[:, sl_up], preferred_element_type=F32)
        a = ((gate * jax.nn.sigmoid(gate)) * up).astype(BF16)
        return jnp.dot(a, wd_ref[sl, :], preferred_element_type=F32)

    def finish(t, x, f):
        o_ref[rows(t), :] = x + _rms(f, gqf_ref[...])

    ys = [out_proj(t) for t in range(n_sub)]
    x, hb = mid_norms(0, ys[0])
    done = None
    for t in range(n_sub):
        f = None
        nxt = None
        for c in range(n_chunks):
            d = ffn_chunk(hb, c)
            f = d if f is None else f + d
            if c == 0 and done is not None:
                finish(*done)
            if c == n_chunks // 2 and t + 1 < n_sub:
                nxt = mid_norms(t + 1, ys[t + 1])
        done = (t, x, f)
        if nxt is not None:
            x, hb = nxt
    finish(*done)


def _outproj_ffn(x, yc, ya, layer, wo, gpm, gpf, wi, wd, gqf):
    t, d = x.shape
    tm = TM_FFN
    row = lambda i: (i, 0)
    return pl.pallas_call(
        _ffn_kernel,
        grid=(t // tm,),
        in_specs=[
            pl.BlockSpec((tm, d), row),
            pl.BlockSpec((tm, CONV_WIDTH), row),
            pl.BlockSpec((tm, ATTN_WIDTH), row),
            _whole_spec(wo, 1),
            _layer_spec(gpm, layer, 1),
            _layer_spec(gpf, layer, 1),
            _whole_spec(wi, 1),
            _whole_spec(wd, 1),
            _layer_spec(gqf, layer, 1),
        ],
        out_specs=pl.BlockSpec((tm, d), row),
        out_shape=jax.ShapeDtypeStruct((t, d), F32),
        compiler_params=pltpu.CompilerParams(
            dimension_semantics=("arbitrary",), vmem_limit_bytes=VMEM_LIMIT),
        name="outproj_ffn",
    )(x, yc, ya, wo, gpm, gpf, wi, wd, gqf)


def kernel(x, w_in, w_conv, rel_bias, g_conv_out, g_attn_out, w_out, g_pre_mix, g_post_mix,
           g_pre_ffn, g_post_ffn, w_ffn_in, w_ffn_out):
    b, s, d = x.shape
    depth = w_in.shape[0]
    tabs = _bias_tables(rel_bias)
    w_conv_t = jnp.swapaxes(w_conv, 1, 2)
    row3 = lambda a: a.reshape(depth, 1, a.shape[-1])
    g_conv_out, g_attn_out, g_pre_mix, g_post_mix, g_pre_ffn, g_post_ffn = map(
        row3, (g_conv_out, g_attn_out, g_pre_mix, g_post_mix, g_pre_ffn, g_post_ffn))
    w_in_b = w_in[0].astype(BF16)
    for l in range(depth):
        yc, q, k, v = _inproj(x, l, g_pre_mix, w_in_b, w_conv_t, g_conv_out)
        casts = [(w_out, l), (w_ffn_in, l), (w_ffn_out, l)]
        if l + 1 < depth:
            casts.append((w_in, l + 1))
        ya, cast = _attention(q, k, v, l, tabs, g_attn_out, casts)
        x = _outproj_ffn(
            x.reshape(b * s, d), yc.reshape(b * s, CONV_WIDTH), ya.reshape(b * s, ATTN_WIDTH), l,
            cast[0], g_post_mix, g_pre_ffn, cast[1], cast[2], g_post_ffn).reshape(b, s, d)
        if l + 1 < depth:
            w_in_b = cast[3]
    return x
```

```python
import functools
import math

import jax
import jax.numpy as jnp
import numpy as np
from jax import lax
from jax.experimental import pallas as pl
from jax.experimental.pallas import tpu as pltpu

D_MODEL = 1024
CHUNK = 64
N_LEFT_CHUNKS = 8
LEFT_PAD = N_LEFT_CHUNKS * CHUNK
CONV_WIDTH = 512
ATTN_WIDTH = 512
HEAD_DIM = 64
N_HEADS = 8
CONV_K = 3
REL_CLIP = 128
D_FF = 2816
EPS = 1e-6
NEG_INF = -1e30
LOG2E = math.log2(math.e)

LANES = 128
TM_PROJ = 1024
TM_FFN = 1024
SUB_ROWS = 512
TQ = 256
N_KBLK = LEFT_PAD // TQ + 1
NK = N_KBLK * TQ
Q_BLOCKS = 4
TQS = Q_BLOCKS * TQ
SCORE_LEAD = 2
BF16_ROW_TILE = 16
TABLE_WIDTH = 1024
FF_CHUNK = 256
VMEM_LIMIT = 56 * 1024 * 1024

F32 = jnp.float32
BF16 = jnp.bfloat16


def _rms(x, g):
    return (x * lax.rsqrt(jnp.mean(x * x, axis=-1, keepdims=True) + EPS)) * g


def _pair_group_norm(y, g):
    lane = lax.broadcasted_iota(jnp.int32, y.shape, 1)
    lo = lane < HEAD_DIM
    sq = y * y
    ms_lo = jnp.sum(jnp.where(lo, sq, 0.0), axis=-1, keepdims=True) * (1.0 / HEAD_DIM)
    ms_hi = jnp.sum(jnp.where(lo, 0.0, sq), axis=-1, keepdims=True) * (1.0 / HEAD_DIM)
    inv = jnp.where(lo, lax.rsqrt(ms_lo + EPS), lax.rsqrt(ms_hi + EPS))
    return (y * inv) * g


def _layer_spec(a, layer, n_grid, single_buffer=False):
    zeros = (0,) * (a.ndim - 1)
    index = {1: lambda i: (layer,) + zeros, 2: lambda bi, i: (layer,) + zeros}[n_grid]
    mode = pl.Buffered(1) if single_buffer else None
    return pl.BlockSpec((None,) + a.shape[1:], index, pipeline_mode=mode)


def _whole_spec(a, n_grid):
    index = {1: lambda i: (0, 0), 2: lambda bi, i: (0, 0)}[n_grid]
    return pl.BlockSpec(a.shape, index, pipeline_mode=pl.Buffered(1))


def _inproj_kernel(x_ref, g_ref, w_ref, wc_ref, gco_ref, yc_ref, q_ref, k_ref, v_ref, carry_ref):
    @pl.when(pl.program_id(1) == 0)
    def _():
        carry_ref[...] = jnp.zeros_like(carry_ref)

    n_sub = x_ref.shape[1] // SUB_ROWS
    lo = lax.broadcasted_iota(jnp.int32, (SUB_ROWS, LANES), 1) < HEAD_DIM
    row8 = lax.broadcasted_iota(jnp.int32, (8, 1), 0)

    def sub_tile(t, hb, prev):
        rows = slice(t * SUB_ROWS, (t + 1) * SUB_ROWS)

        def project(j):
            return jnp.dot(hb, w_ref[:, j * CONV_WIDTH:(j + 1) * CONV_WIDTH], preferred_element_type=F32)

        h = project(0)
        u = project(2) * h
        u1 = pltpu.roll(u, 1, 0)
        u2 = pltpu.roll(u, 2, 0)
        u1 = jnp.concatenate([jnp.where(row8 == 0, prev[7:8], u1[:8]), u1[8:]], axis=0)
        u2 = jnp.concatenate(
            [jnp.where(row8 == 0, prev[6:7], jnp.where(row8 == 1, prev[7:8], u2[:8])), u2[8:]], axis=0)
        conv = project(1) * (u2 * wc_ref[0:1] + u1 * wc_ref[1:2] + u * wc_ref[2:3])
        for j in range(CONV_WIDTH // LANES):
            sl = slice(j * LANES, (j + 1) * LANES)
            yc_ref[0, rows, sl] = _pair_group_norm(conv[:, sl], gco_ref[:, sl]).astype(BF16)

        def head_groups(ref, val, fill):
            for pair in range(N_HEADS // 2):
                sl = slice(pair * LANES, (pair + 1) * LANES)
                first = slice(2 * pair * LANES, (2 * pair + 1) * LANES)
                second = slice((2 * pair + 1) * LANES, (2 * pair + 2) * LANES)
                ref[0, rows, first] = jnp.where(lo, val[:, sl], fill).astype(BF16)
                ref[0, rows, second] = jnp.where(lo, fill, val[:, sl]).astype(BF16)

        head_groups(q_ref, project(3) * (LOG2E * HEAD_DIM ** -0.5), 0.0)
        k_ref[0, rows, :] = project(4).astype(BF16)
        head_groups(v_ref, project(5), 1.0)
        return u[SUB_ROWS - 8:]

    prev = carry_ref[...]
    for t in range(n_sub):
        hb = _rms(x_ref[0, t * SUB_ROWS:(t + 1) * SUB_ROWS, :], g_ref[...]).astype(BF16)
        prev = sub_tile(t, hb, prev)
    carry_ref[...] = prev


def _inproj(x, layer, g, w, wc, gco):
    b, s, d = x.shape
    tm = TM_PROJ
    row = lambda bi, i: (bi, i, 0)
    narrow = jax.ShapeDtypeStruct((b, s, CONV_WIDTH), BF16)
    wide = jax.ShapeDtypeStruct((b, s, N_HEADS * LANES), BF16)
    narrow_spec = pl.BlockSpec((1, tm, CONV_WIDTH), row)
    wide_spec = pl.BlockSpec((1, tm, N_HEADS * LANES), row)
    return pl.pallas_call(
        _inproj_kernel,
        grid=(b, s // tm),
        in_specs=[
            pl.BlockSpec((1, tm, d), row),
            _layer_spec(g, layer, 2),
            _whole_spec(w, 2),
            _layer_spec(wc, layer, 2),
            _layer_spec(gco, layer, 2),
        ],
        out_specs=[narrow_spec, wide_spec, narrow_spec, wide_spec],
        out_shape=[narrow, wide, narrow, wide],
        scratch_shapes=[pltpu.VMEM((8, CONV_WIDTH), F32)],
        compiler_params=pltpu.CompilerParams(
            dimension_semantics=("arbitrary", "arbitrary"), vmem_limit_bytes=VMEM_LIMIT),
        name="inproj_conv",
    )(x, g, w, wc, gco)


def _table_kernel(f_ref, o_ref):
    qc = lax.broadcasted_iota(jnp.int32, (TQ, NK), 0) // CHUNK
    kc = lax.broadcasted_iota(jnp.int32, (TQ, NK), 1) // CHUNK
    valid = (kc >= qc) & (kc <= qc + N_LEFT_CHUNKS)
    for head in range(N_HEADS):
        f = jnp.broadcast_to(f_ref[head] - f_ref[head][:, 0:1], (TQ, TABLE_WIDTH))
        r = pltpu.roll(f, 0, 1, stride=1, stride_axis=0)[:, :NK]
        o_ref[head] = jnp.where(valid, r * LOG2E, NEG_INF)


def _bias_tables(rel_bias):
    depth, h, _ = rel_bias.shape
    m = jnp.arange(TABLE_WIDTH)
    m = jnp.where(m < NK, m, m - TABLE_WIDTH)
    idx = jnp.clip(LEFT_PAD - m, -REL_CLIP, REL_CLIP) + REL_CLIP
    f = rel_bias.astype(F32)[:, :, idx].reshape(depth, h, 1, TABLE_WIDTH)
    return pl.pallas_call(
        _table_kernel,
        grid=(depth,),
        in_specs=[pl.BlockSpec((None, h, 1, TABLE_WIDTH), lambda i: (i, 0, 0, 0))],
        out_specs=pl.BlockSpec((None, h, TQ, NK), lambda i: (i, 0, 0, 0)),
        out_shape=jax.ShapeDtypeStruct((depth, h, TQ, NK), F32),
        name="bias_tables",
    )(f)


def _table_tile_classes():
    qi = np.arange(TQ)[:, None]
    kj = np.arange(NK)[None, :]
    valid = (kj // CHUNK >= qi // CHUNK) & (kj // CHUNK <= qi // CHUNK + N_LEFT_CHUNKS)
    far = (LEFT_PAD + qi - kj) >= REL_CLIP
    classes = {}
    for qc in range(TQ // CHUNK):
        for ct in range(NK // LANES):
            blk = (slice(qc * CHUNK, (qc + 1) * CHUNK), slice(ct * LANES, (ct + 1) * LANES))
            if not valid[blk].any():
                classes[qc, ct] = "dead"
            elif valid[blk].all() and far[blk].all():
                classes[qc, ct] = "zero"
            else:
                classes[qc, ct] = "add"
    return classes


TILE_CLASS = _table_tile_classes()


def _attn_kernel(q_ref, kp_ref, kc_ref, vp_ref, vc_ref, tab_ref, g_ref, *rest):
    n_cast = len(rest) // 2
    o_ref = rest[n_cast]
    for src, dst in zip(rest[:n_cast], rest[n_cast + 1:]):
        dst[...] = src[...].astype(BF16)

    i = pl.program_id(1)
    lo = lax.broadcasted_iota(jnp.int32, (TQ, LANES), 1) < HEAD_DIM
    nt = (((1,), (1,)), ((), ()))

    def attend(first_step):
        def key_parts(j):
            lo_row = (j + 1 - N_KBLK) * TQ
            parts = []
            if lo_row < 0 and not first_step:
                parts.append((kp_ref, vp_ref, LEFT_PAD + lo_row, LEFT_PAD))
            parts.append((kc_ref, vc_ref, max(lo_row, 0), (j + 1) * TQ))
            return parts

        def scores(unit):
            j, head = unit
            pair = slice(head // 2 * LANES, (head // 2 + 1) * LANES)
            hl = slice(head * LANES, (head + 1) * LANES)
            q = q_ref[0, j * TQ:(j + 1) * TQ, hl]
            s = jnp.concatenate(
                [lax.dot_general(q, k[0, r0:r1, pair], nt, preferred_element_type=F32)
                 for k, _, r0, r1 in key_parts(j)], axis=1)
            ct0 = (NK - s.shape[1]) // LANES
            out = []
            for qc in range(TQ // CHUNK):
                rows = slice(qc * CHUNK, (qc + 1) * CHUNK)
                tiles = []
                for ct in range(ct0, NK // LANES):
                    cls = TILE_CLASS[qc, ct]
                    if cls == "dead":
                        continue
                    tile = s[rows, (ct - ct0) * LANES:(ct - ct0 + 1) * LANES]
                    if cls == "add":
                        tile = tile + tab_ref[head, rows, ct * LANES:(ct + 1) * LANES]
                    tiles.append((ct, tile))
                mx = jnp.max(functools.reduce(jnp.maximum, [t for _, t in tiles]), axis=-1, keepdims=True)
                out.append((tiles, mx))
            return ct0, out

        def weighted_values(unit, ct0, chunks):
            j, head = unit
            hl = slice(head * LANES, (head + 1) * LANES)
            p_rows = []
            for tiles, mx in chunks:
                live = dict(tiles)
                p_rows.append(jnp.concatenate(
                    [jnp.exp2(live[ct] - mx).astype(BF16) if ct in live else jnp.zeros((CHUNK, LANES), BF16)
                     for ct in range(ct0, NK // LANES)], axis=1))
            p = jnp.concatenate(p_rows, axis=0)
            acc, col = None, 0
            for _, v, r0, r1 in key_parts(j):
                part = jnp.dot(p[:, col:col + r1 - r0], v[0, r0:r1, hl], preferred_element_type=F32)
                acc = part if acc is None else acc + part
                col += r1 - r0
            return acc

        units = [(j, h) for j in range(Q_BLOCKS) for h in range(N_HEADS)]
        pending = [scores(u) for u in units[:SCORE_LEAD]]
        acc = []
        for n, unit in enumerate(units):
            if n + SCORE_LEAD < len(units):
                pending.append(scores(units[n + SCORE_LEAD]))
            acc.append(weighted_values(unit, *pending.pop(0)))
            j, head = unit
            if head % 2 == 1:
                sl = slice(head // 2 * LANES, (head // 2 + 1) * LANES)
                a0, a1 = acc
                acc = []
                y = jnp.where(lo, a0 / pltpu.roll(a0, HEAD_DIM, 1), a1 / pltpu.roll(a1, HEAD_DIM, 1))
                o_ref[0, j * TQ:(j + 1) * TQ, sl] = _pair_group_norm(y, g_ref[:, sl]).astype(BF16)

    @pl.when(i == 0)
    def _():
        attend(True)

    @pl.when(i > 0)
    def _():
        attend(False)


def _attention(q, k, v, layer, tab, g, casts):
    b, s, w = k.shape
    n_i = s // TQS
    n_steps = b * n_i
    cur = lambda bi, i: (bi, i, 0)
    prev = lambda bi, i: (bi, jnp.maximum(i * (TQS // LEFT_PAD) - 1, 0), 0)
    cast_in, cast_out, cast_shape = [], [], []
    for a, lyr in casts:
        _, r, c = a.shape
        per = 1
        while r * per % (n_steps * BF16_ROW_TILE):
            per *= 2
        rows = r * per // n_steps
        cast_in.append(pl.BlockSpec(
            (None, rows, c), lambda bi, i, lyr=lyr, per=per: (lyr, (bi * n_i + i) // per, 0)))
        cast_out.append(pl.BlockSpec((rows, c), lambda bi, i, per=per: ((bi * n_i + i) // per, 0)))
        cast_shape.append(jax.ShapeDtypeStruct((r, c), BF16))
    out = pl.pallas_call(
        _attn_kernel,
        grid=(b, n_i),
        in_specs=[
            pl.BlockSpec((1, TQS, q.shape[-1]), cur),
            pl.BlockSpec((1, LEFT_PAD, w), prev),
            pl.BlockSpec((1, TQS, w), cur),
            pl.BlockSpec((1, LEFT_PAD, v.shape[-1]), prev),
            pl.BlockSpec((1, TQS, v.shape[-1]), cur),
            _layer_spec(tab, layer, 2, single_buffer=True),
            _layer_spec(g, layer, 2),
        ] + cast_in,
        out_specs=[pl.BlockSpec((1, TQS, w), cur)] + cast_out,
        out_shape=[jax.ShapeDtypeStruct((b, s, w), BF16)] + cast_shape,
        compiler_params=pltpu.CompilerParams(
            dimension_semantics=("arbitrary", "arbitrary"), vmem_limit_bytes=VMEM_LIMIT),
        name="band_attention",
    )(q, k, k, v, v, tab, g, *[a for a, _ in casts])
    return out[0], out[1:]


def _ffn_kernel(x_ref, yc_ref, ya_ref, wo_ref, gpm_ref, gpf_ref, wi_ref, wd_ref, gqf_ref, o_ref):
    n_sub = x_ref.shape[0] // SUB_ROWS
    n_chunks = D_FF // FF_CHUNK

    def rows(t):
        return slice(t * SUB_ROWS, (t + 1) * SUB_ROWS)

    def out_proj(t):
        y = jnp.dot(yc_ref[rows(t), :], wo_ref[0:CONV_WIDTH], preferred_element_type=F32)
        return y + jnp.dot(ya_ref[rows(t), :], wo_ref[CONV_WIDTH:], preferred_element_type=F32)

    def mid_norms(t, y):
        x = x_ref[rows(t), :] + _rms(y, gpm_ref[...])
        return x, _rms(x, gpf_ref[...]).astype(BF16)

    def ffn_chunk(hb, c):
        sl = slice(c * FF_CHUNK, (c + 1) * FF_CHUNK)
        sl_up = slice(D_FF + c * FF_CHUNK, D_FF + (c + 1) * FF_CHUNK)
        gate = jnp.dot(hb, wi_ref[:, sl], preferred_element_type=F32)
        up = jnp.dot(hb, wi_ref[:, sl_up], preferred_element_type=F32)
        a = ((gate * jax.nn.sigmoid(gate)) * up).astype(BF16)
        return jnp.dot(a, wd_ref[sl, :], preferred_element_type=F32)

    def finish(t, x, f):
        o_ref[rows(t), :] = x + _rms(f, gqf_ref[...])

    ys = [out_proj(t) for t in range(n_sub)]
    x, hb = mid_norms(0, ys[0])
    done = None
    for t in range(n_sub):
        f = None
        nxt = None
        for c in range(n_chunks):
            d = ffn_chunk(hb, c)
            f = d if f is None else f + d
            if c == 0 and done is not None:
                finish(*done)
            if c == n_chunks // 2 and t + 1 < n_sub:
                nxt = mid_norms(t + 1, ys[t + 1])
        done = (t, x, f)
        if nxt is not None:
            x, hb = nxt
    finish(*done)


def _outproj_ffn(x, yc, ya, layer, wo, gpm, gpf, wi, wd, gqf):
    t, d = x.shape
    tm = TM_FFN
    row = lambda i: (i, 0)
    return pl.pallas_call(
        _ffn_kernel,
        grid=(t // tm,),
        in_specs=[
            pl.BlockSpec((tm, d), row),
            pl.BlockSpec((tm, CONV_WIDTH), row),
            pl.BlockSpec((tm, ATTN_WIDTH), row),
            _whole_spec(wo, 1),
            _layer_spec(gpm, layer, 1),
            _layer_spec(gpf, layer, 1),
            _whole_spec(wi, 1),
            _whole_spec(wd, 1),
            _layer_spec(gqf, layer, 1),
        ],
        out_specs=pl.BlockSpec((tm, d), row),
        out_shape=jax.ShapeDtypeStruct((t, d), F32),
        compiler_params=pltpu.CompilerParams(
            dimension_semantics=("arbitrary",), vmem_limit_bytes=VMEM_LIMIT),
        name="outproj_ffn",
    )(x, yc, ya, wo, gpm, gpf, wi, wd, gqf)


def kernel(x, w_in, w_conv, rel_bias, g_conv_out, g_attn_out, w_out, g_pre_mix, g_post_mix,
           g_pre_ffn, g_post_ffn, w_ffn_in, w_ffn_out):
    b, s, d = x.shape
    depth = w_in.shape[0]
    tabs = _bias_tables(rel_bias)
    w_conv_t = jnp.swapaxes(w_conv, 1, 2)
    row3 = lambda a: a.reshape(depth, 1, a.shape[-1])
    g_conv_out, g_attn_out, g_pre_mix, g_post_mix, g_pre_ffn, g_post_ffn = map(
        row3, (g_conv_out, g_attn_out, g_pre_mix, g_post_mix, g_pre_ffn, g_post_ffn))
    w_in_b = w_in[0].astype(BF16)
    for l in range(depth):
        yc, q, k, v = _inproj(x, l, g_pre_mix, w_in_b, w_conv_t, g_conv_out)
        casts = [(w_out, l), (w_ffn_in, l), (w_ffn_out, l)]
        if l + 1 < depth:
            casts.append((w_in, l + 1))
        ya, cast = _attention(q, k, v, l, tabs, g_attn_out, casts)
        x = _outproj_ffn(
            x.reshape(b * s, d), yc.reshape(b * s, CONV_WIDTH), ya.reshape(b * s, ATTN_WIDTH), l,
            cast[0], g_post_mix, g_pre_ffn, cast[1], cast[2], g_post_ffn).reshape(b, s, d)
        if l + 1 < depth:
            w_in_b = cast[3]
    return x
```

```python
import functools
import math

import jax
import jax.numpy as jnp
import numpy as np
from jax import lax
from jax.experimental import pallas as pl
from jax.experimental.pallas import tpu as pltpu

D_MODEL = 1024
CHUNK = 64
N_LEFT_CHUNKS = 8
LEFT_PAD = N_LEFT_CHUNKS * CHUNK
CONV_WIDTH = 512
ATTN_WIDTH = 512
HEAD_DIM = 64
N_HEADS = 8
CONV_K = 3
REL_CLIP = 128
D_FF = 2816
EPS = 1e-6
NEG_INF = -1e30
LOG2E = math.log2(math.e)

LANES = 128
TM_FFN = 1024
SUB_ROWS = 512
TQ = 256
N_KBLK = LEFT_PAD // TQ + 1
NK = N_KBLK * TQ
Q_BLOCKS = 4
TQS = Q_BLOCKS * TQ
SCORE_LEAD = 2
BF16_ROW_TILE = 16
TABLE_WIDTH = 1024
FF_CHUNK = 256
VMEM_LIMIT = 56 * 1024 * 1024

F32 = jnp.float32
BF16 = jnp.bfloat16


def _rms(x, g):
    return (x * lax.rsqrt(jnp.mean(x * x, axis=-1, keepdims=True) + EPS)) * g


def _pair_group_norm(y, g):
    lane = lax.broadcasted_iota(jnp.int32, y.shape, 1)
    lo = lane < HEAD_DIM
    sq = y * y
    ms_lo = jnp.sum(jnp.where(lo, sq, 0.0), axis=-1, keepdims=True) * (1.0 / HEAD_DIM)
    ms_hi = jnp.sum(jnp.where(lo, 0.0, sq), axis=-1, keepdims=True) * (1.0 / HEAD_DIM)
    inv = jnp.where(lo, lax.rsqrt(ms_lo + EPS), lax.rsqrt(ms_hi + EPS))
    return (y * inv) * g


def _layer_spec(a, layer, n_grid, single_buffer=False):
    zeros = (0,) * (a.ndim - 1)
    index = {1: lambda i: (layer,) + zeros, 2: lambda bi, i: (layer,) + zeros}[n_grid]
    mode = pl.Buffered(1) if single_buffer else None
    return pl.BlockSpec((None,) + a.shape[1:], index, pipeline_mode=mode)


def _whole_spec(a, n_grid):
    index = {1: lambda i: (0, 0), 2: lambda bi, i: (0, 0)}[n_grid]
    return pl.BlockSpec(a.shape, index, pipeline_mode=pl.Buffered(1))


def _table_kernel(f_ref, o_ref):
    qc = lax.broadcasted_iota(jnp.int32, (TQ, NK), 0) // CHUNK
    kc = lax.broadcasted_iota(jnp.int32, (TQ, NK), 1) // CHUNK
    valid = (kc >= qc) & (kc <= qc + N_LEFT_CHUNKS)
    for head in range(N_HEADS):
        f = jnp.broadcast_to(f_ref[head] - f_ref[head][:, 0:1], (TQ, TABLE_WIDTH))
        r = pltpu.roll(f, 0, 1, stride=1, stride_axis=0)[:, :NK]
        o_ref[head] = jnp.where(valid, r * LOG2E, NEG_INF)


def _bias_tables(rel_bias):
    depth, h, _ = rel_bias.shape
    m = jnp.arange(TABLE_WIDTH)
    m = jnp.where(m < NK, m, m - TABLE_WIDTH)
    idx = jnp.clip(LEFT_PAD - m, -REL_CLIP, REL_CLIP) + REL_CLIP
    f = rel_bias.astype(F32)[:, :, idx].reshape(depth, h, 1, TABLE_WIDTH)
    return pl.pallas_call(
        _table_kernel,
        grid=(depth,),
        in_specs=[pl.BlockSpec((None, h, 1, TABLE_WIDTH), lambda i: (i, 0, 0, 0))],
        out_specs=pl.BlockSpec((None, h, TQ, NK), lambda i: (i, 0, 0, 0)),
        out_shape=jax.ShapeDtypeStruct((depth, h, TQ, NK), F32),
        name="bias_tables",
    )(f)


def _table_tile_classes():
    qi = np.arange(TQ)[:, None]
    kj = np.arange(NK)[None, :]
    valid = (kj // CHUNK >= qi // CHUNK) & (kj // CHUNK <= qi // CHUNK + N_LEFT_CHUNKS)
    far = (LEFT_PAD + qi - kj) >= REL_CLIP
    classes = {}
    for qc in range(TQ // CHUNK):
        for ct in range(NK // LANES):
            blk = (slice(qc * CHUNK, (qc + 1) * CHUNK), slice(ct * LANES, (ct + 1) * LANES))
            if not valid[blk].any():
                classes[qc, ct] = "dead"
            elif valid[blk].all() and far[blk].all():
                classes[qc, ct] = "zero"
            else:
                classes[qc, ct] = "add"
    return classes


TILE_CLASS = _table_tile_classes()


def _mixer_kernel(x_ref, gpre_ref, w_ref, wc_ref, gco_ref, tab_ref, gat_ref, *rest, layer):
    n_cast = (len(rest) - 6) // 2
    yc_ref, ya_ref = rest[n_cast:n_cast + 2]
    carry_ref, q_s, k_s, v_s = rest[-4:]
    for src, dst in zip(rest[:n_cast], rest[n_cast + 2:2 * n_cast + 2]):
        dst[...] = src[...].astype(BF16)

    i = pl.program_id(1)
    g_pre = gpre_ref[layer:layer + 1, :]
    g_conv = gco_ref[layer:layer + 1, :]
    g_attn = gat_ref[layer:layer + 1, :]

    @pl.when(i == 0)
    def _():
        carry_ref[...] = jnp.zeros_like(carry_ref)

    @pl.when(i > 0)
    def _():
        k_s[0:LEFT_PAD] = k_s[TQS:TQS + LEFT_PAD]
        v_s[0:LEFT_PAD] = v_s[TQS:TQS + LEFT_PAD]

    lo_sub = lax.broadcasted_iota(jnp.int32, (SUB_ROWS, LANES), 1) < HEAD_DIM
    lo = lax.broadcasted_iota(jnp.int32, (TQ, LANES), 1) < HEAD_DIM
    row8 = lax.broadcasted_iota(jnp.int32, (8, 1), 0)
    nt = (((1,), (1,)), ((), ()))

    def inproj_pieces(t, state):
        rows = slice(t * SUB_ROWS, (t + 1) * SUB_ROWS)
        kv_rows = slice(LEFT_PAD + t * SUB_ROWS, LEFT_PAD + (t + 1) * SUB_ROWS)

        def project(j):
            return jnp.dot(state["hb"], w_ref[:, j * CONV_WIDTH:(j + 1) * CONV_WIDTH],
                           preferred_element_type=F32)

        def norm():
            state["hb"] = _rms(x_ref[0, rows, :], g_pre).astype(BF16)

        def gate():
            state["u"] = project(2) * project(0)

        def conv():
            u, prev = state["u"], state["prev"]
            u1 = pltpu.roll(u, 1, 0)
            u2 = pltpu.roll(u, 2, 0)
            u1 = jnp.concatenate([jnp.where(row8 == 0, prev[7:8], u1[:8]), u1[8:]], axis=0)
            u2 = jnp.concatenate(
                [jnp.where(row8 == 0, prev[6:7], jnp.where(row8 == 1, prev[7:8], u2[:8])), u2[8:]], axis=0)
            y = project(1) * (u2 * wc_ref[0:1] + u1 * wc_ref[1:2] + u * wc_ref[2:3])
            for j in range(CONV_WIDTH // LANES):
                sl = slice(j * LANES, (j + 1) * LANES)
                yc_ref[0, rows, sl] = _pair_group_norm(y[:, sl], g_conv[:, sl]).astype(BF16)
            state["prev"] = u[SUB_ROWS - 8:]

        def head_groups(ref, ref_rows, val, fill):
            for pair in range(N_HEADS // 2):
                sl = slice(pair * LANES, (pair + 1) * LANES)
                first = slice(2 * pair * LANES, (2 * pair + 1) * LANES)
                second = slice((2 * pair + 1) * LANES, (2 * pair + 2) * LANES)
                ref[ref_rows, first] = jnp.where(lo_sub, val[:, sl], fill).astype(BF16)
                ref[ref_rows, second] = jnp.where(lo_sub, fill, val[:, sl]).astype(BF16)

        def query():
            head_groups(q_s, rows, project(3) * (LOG2E * HEAD_DIM ** -0.5), 0.0)

        def key():
            k_s[kv_rows, :] = project(4).astype(BF16)

        def value():
            head_groups(v_s, kv_rows, project(5), 1.0)

        return [norm, gate, conv, query, key, value]

    def attend(blocks, first_step, pieces):
        def key_rows(j):
            lo_row = LEFT_PAD + (j + 1 - N_KBLK) * TQ
            return slice(max(lo_row, LEFT_PAD) if first_step else lo_row, LEFT_PAD + (j + 1) * TQ)

        def scores(unit):
            j, head = unit
            pair = slice(head // 2 * LANES, (head // 2 + 1) * LANES)
            hl = slice(head * LANES, (head + 1) * LANES)
            s = lax.dot_general(q_s[j * TQ:(j + 1) * TQ, hl], k_s[key_rows(j), pair], nt,
                                preferred_element_type=F32)
            ct0 = (NK - s.shape[1]) // LANES
            out = []
            for qc in range(TQ // CHUNK):
                rows = slice(qc * CHUNK, (qc + 1) * CHUNK)
                tiles = []
                for ct in range(ct0, NK // LANES):
                    cls = TILE_CLASS[qc, ct]
                    if cls == "dead":
                        continue
                    tile = s[rows, (ct - ct0) * LANES:(ct - ct0 + 1) * LANES]
                    if cls == "add":
                        tile = tile + tab_ref[head, rows, ct * LANES:(ct + 1) * LANES]
                    tiles.append((ct, tile))
                mx = jnp.max(functools.reduce(jnp.maximum, [t for _, t in tiles]), axis=-1, keepdims=True)
                out.append((tiles, mx))
            return ct0, out

        def weighted_values(unit, ct0, chunks):
            j, head = unit
            hl = slice(head * LANES, (head + 1) * LANES)
            p_rows = []
            for tiles, mx in chunks:
                live = dict(tiles)
                p_rows.append(jnp.concatenate(
                    [jnp.exp2(live[ct] - mx).astype(BF16) if ct in live else jnp.zeros((CHUNK, LANES), BF16)
                     for ct in range(ct0, NK // LANES)], axis=1))
            p = jnp.concatenate(p_rows, axis=0)
            return jnp.dot(p, v_s[key_rows(j), hl], preferred_element_type=F32)

        units = [(j, h) for j in blocks for h in range(N_HEADS)]
        piece_at = {round(n * len(units) / len(pieces)): p for n, p in enumerate(pieces)} if pieces else {}
        pending = [scores(u) for u in units[:SCORE_LEAD]]
        acc = []
        for n, unit in enumerate(units):
            if n in piece_at:
                piece_at[n]()
            if n + SCORE_LEAD < len(units):
                pending.append(scores(units[n + SCORE_LEAD]))
            acc.append(weighted_values(unit, *pending.pop(0)))
            j, head = unit
            if head % 2 == 1:
                sl = slice(head // 2 * LANES, (head // 2 + 1) * LANES)
                a0, a1 = acc
                acc = []
                y = jnp.where(lo, a0 / pltpu.roll(a0, HEAD_DIM, 1), a1 / pltpu.roll(a1, HEAD_DIM, 1))
                ya_ref[0, j * TQ:(j + 1) * TQ, sl] = _pair_group_norm(y, g_attn[:, sl]).astype(BF16)

    n_sub = TQS // SUB_ROWS
    blocks_per_sub = SUB_ROWS // TQ
    state = {"prev": carry_ref[...]}
    for piece in inproj_pieces(0, state):
        piece()
    for t in range(n_sub):
        blocks = range(t * blocks_per_sub, (t + 1) * blocks_per_sub)
        carry_ref[...] = state["prev"]

        def both(first_step, t=t, blocks=blocks):
            st = {"prev": carry_ref[...]}
            attend(blocks, first_step, inproj_pieces(t + 1, st) if t + 1 < n_sub else [])
            if t + 1 < n_sub:
                carry_ref[...] = st["prev"]

        if t == 0:
            pl.when(i == 0)(lambda: both(True))
            pl.when(i > 0)(lambda: both(False))
        else:
            both(False)
        state = {"prev": carry_ref[...]}


def _mixer(x, layer, gpre, w, wc, gco, tab, gat, casts):
    b, s, d = x.shape
    n_i = s // TQS
    n_steps = b * n_i
    cur = lambda bi, i: (bi, i, 0)
    cast_in, cast_out, cast_shape = [], [], []
    for a, lyr in casts:
        _, r, c = a.shape
        per = 1
        while r * per % (n_steps * BF16_ROW_TILE):
            per *= 2
        rows = r * per // n_steps
        cast_in.append(pl.BlockSpec(
            (None, rows, c), lambda bi, i, lyr=lyr, per=per: (lyr, (bi * n_i + i) // per, 0)))
        cast_out.append(pl.BlockSpec((rows, c), lambda bi, i, per=per: ((bi * n_i + i) // per, 0)))
        cast_shape.append(jax.ShapeDtypeStruct((r, c), BF16))
    branch = jax.ShapeDtypeStruct((b, s, CONV_WIDTH), BF16)
    branch_spec = pl.BlockSpec((1, TQS, CONV_WIDTH), cur)
    out = pl.pallas_call(
        functools.partial(_mixer_kernel, layer=layer),
        grid=(b, n_i),
        in_specs=[
            pl.BlockSpec((1, TQS, d), cur),
            _whole_spec(gpre, 2),
            _whole_spec(w, 2),
            _layer_spec(wc, layer, 2),
            _whole_spec(gco, 2),
            _layer_spec(tab, layer, 2, single_buffer=True),
            _whole_spec(gat, 2),
        ] + cast_in,
        out_specs=[branch_spec, branch_spec] + cast_out,
        out_shape=[branch, branch] + cast_shape,
        scratch_shapes=[
            pltpu.VMEM((8, CONV_WIDTH), F32),
            pltpu.VMEM((TQS, N_HEADS * LANES), BF16),
            pltpu.VMEM((LEFT_PAD + TQS, ATTN_WIDTH), BF16),
            pltpu.VMEM((LEFT_PAD + TQS, N_HEADS * LANES), BF16),
        ],
        compiler_params=pltpu.CompilerParams(
            dimension_semantics=("arbitrary", "arbitrary"), vmem_limit_bytes=VMEM_LIMIT),
        name="token_mixer",
    )(x, gpre, w, wc, gco, tab, gat, *[a for a, _ in casts])
    return out[0], out[1], out[2:]


def _ffn_kernel(x_ref, yc_ref, ya_ref, wo_ref, gpm_ref, gpf_ref, wi_ref, wd_ref, gqf_ref, o_ref, *, layer):
    n_sub = x_ref.shape[0] // SUB_ROWS
    n_chunks = D_FF // FF_CHUNK
    g_post_mix = gpm_ref[layer:layer + 1, :]
    g_pre_ffn = gpf_ref[layer:layer + 1, :]
    g_post_ffn = gqf_ref[layer:layer + 1, :]

    def rows(t):
        return slice(t * SUB_ROWS, (t + 1) * SUB_ROWS)

    def out_proj(t):
        y = jnp.dot(yc_ref[rows(t), :], wo_ref[0:CONV_WIDTH], preferred_element_type=F32)
        return y + jnp.dot(ya_ref[rows(t), :], wo_ref[CONV_WIDTH:], preferred_element_type=F32)

    def mid_norms(t, y):
        x = x_ref[rows(t), :] + _rms(y, g_post_mix)
        return x, _rms(x, g_pre_ffn).astype(BF16)

    def ffn_chunk(hb, c):
        sl = slice(c * FF_CHUNK, (c + 1) * FF_CHUNK)
        sl_up = slice(D_FF + c * FF_CHUNK, D_FF + (c + 1) * FF_CHUNK)
        gate = jnp.dot(hb, wi_ref[:, sl], preferred_element_type=F32)
        up = jnp.dot(hb, wi_ref[:, sl_up], preferred_element_type=F32)
        a = ((gate * jax.nn.sigmoid(gate)) * up).astype(BF16)
        return jnp.dot(a, wd_ref[sl, :], preferred_element_type=F32)

    def finish(t, x, f):
        o_ref[rows(t), :] = x + _rms(f, g_post_ffn)

    ys = [out_proj(t) for t in range(n_sub)]
    x, hb = mid_norms(0, ys[0])
    done = None
    for t in range(n_sub):
        f = None
        nxt = None
        for c in range(n_chunks):
            d = ffn_chunk(hb, c)
            f = d if f is None else f + d
            if c == 0 and done is not None:
                finish(*done)
            if c == n_chunks // 2 and t + 1 < n_sub:
                nxt = mid_norms(t + 1, ys[t + 1])
        done = (t, x, f)
        if nxt is not None:
            x, hb = nxt
    finish(*done)


def _outproj_ffn(x, yc, ya, layer, wo, gpm, gpf, wi, wd, gqf):
    t, d = x.shape
    tm = TM_FFN
    row = lambda i: (i, 0)
    return pl.pallas_call(
        functools.partial(_ffn_kernel, layer=layer),
        grid=(t // tm,),
        in_specs=[
            pl.BlockSpec((tm, d), row),
            pl.BlockSpec((tm, CONV_WIDTH), row),
            pl.BlockSpec((tm, ATTN_WIDTH), row),
            _whole_spec(wo, 1),
            _whole_spec(gpm, 1),
            _whole_spec(gpf, 1),
            _whole_spec(wi, 1),
            _whole_spec(wd, 1),
            _whole_spec(gqf, 1),
        ],
        out_specs=pl.BlockSpec((tm, d), row),
        out_shape=jax.ShapeDtypeStruct((t, d), F32),
        compiler_params=pltpu.CompilerParams(
            dimension_semantics=("arbitrary",), vmem_limit_bytes=VMEM_LIMIT),
        name="outproj_ffn",
    )(x, yc, ya, wo, gpm, gpf, wi, wd, gqf)


def kernel(x, w_in, w_conv, rel_bias, g_conv_out, g_attn_out, w_out, g_pre_mix, g_post_mix,
           g_pre_ffn, g_post_ffn, w_ffn_in, w_ffn_out):
    b, s, d = x.shape
    depth = w_in.shape[0]
    tabs = _bias_tables(rel_bias)
    w_conv_t = jnp.swapaxes(w_conv, 1, 2)
    w_in_b = w_in[0].astype(BF16)
    for l in range(depth):
        casts = [(w_out, l), (w_ffn_in, l), (w_ffn_out, l)]
        if l + 1 < depth:
            casts.append((w_in, l + 1))
        yc, ya, cast = _mixer(x, l, g_pre_mix, w_in_b, w_conv_t, g_conv_out, tabs, g_attn_out, casts)
        x = _outproj_ffn(
            x.reshape(b * s, d), yc.reshape(b * s, CONV_WIDTH), ya.reshape(b * s, ATTN_WIDTH), l,
            cast[0], g_post_mix, g_pre_ffn, cast[1], cast[2], g_post_ffn).reshape(b, s, d)
        if l + 1 < depth:
            w_in_b = cast[3]
    return x
```

```python
import functools
import math

import jax
import jax.numpy as jnp
import numpy as np
from jax import lax
from jax.experimental import pallas as pl
from jax.experimental.pallas import tpu as pltpu

D_MODEL = 1024
CHUNK = 64
N_LEFT_CHUNKS = 8
LEFT_PAD = N_LEFT_CHUNKS * CHUNK
CONV_WIDTH = 512
ATTN_WIDTH = 512
HEAD_DIM = 64
N_HEADS = 8
CONV_K = 3
REL_CLIP = 128
D_FF = 2816
EPS = 1e-6
NEG_INF = -1e30
LOG2E = math.log2(math.e)

LANES = 128
TM_PROJ = 1024
TM_FFN = 1024
SUB_ROWS = 512
TQ = 256
N_KBLK = LEFT_PAD // TQ + 1
NK = N_KBLK * TQ
Q_BLOCKS = 4
TQS = Q_BLOCKS * TQ
SCORE_LEAD = 2
BF16_ROW_TILE = 16
TABLE_WIDTH = 1024
FF_CHUNK = 256
VMEM_LIMIT = 56 * 1024 * 1024

F32 = jnp.float32
BF16 = jnp.bfloat16


def _rms(x, g):
    return (x * lax.rsqrt(jnp.mean(x * x, axis=-1, keepdims=True) + EPS)) * g


def _pair_group_norm(y, g):
    lane = lax.broadcasted_iota(jnp.int32, y.shape, 1)
    lo = lane < HEAD_DIM
    sq = y * y
    ms_lo = jnp.sum(jnp.where(lo, sq, 0.0), axis=-1, keepdims=True) * (1.0 / HEAD_DIM)
    ms_hi = jnp.sum(jnp.where(lo, 0.0, sq), axis=-1, keepdims=True) * (1.0 / HEAD_DIM)
    inv = jnp.where(lo, lax.rsqrt(ms_lo + EPS), lax.rsqrt(ms_hi + EPS))
    return (y * inv) * g


def _layer_spec(a, layer, n_grid, single_buffer=False):
    zeros = (0,) * (a.ndim - 1)
    index = {1: lambda i: (layer,) + zeros, 2: lambda bi, i: (layer,) + zeros}[n_grid]
    mode = pl.Buffered(1) if single_buffer else None
    return pl.BlockSpec((None,) + a.shape[1:], index, pipeline_mode=mode)


def _whole_spec(a, n_grid):
    index = {1: lambda i: (0, 0), 2: lambda bi, i: (0, 0)}[n_grid]
    return pl.BlockSpec(a.shape, index, pipeline_mode=pl.Buffered(1))


def _inproj_kernel(x_ref, g_ref, w_ref, wc_ref, gco_ref, yc_ref, q_ref, k_ref, v_ref, carry_ref, *, layer):
    @pl.when(pl.program_id(1) == 0)
    def _():
        carry_ref[...] = jnp.zeros_like(carry_ref)

    n_sub = x_ref.shape[1] // SUB_ROWS
    lo = lax.broadcasted_iota(jnp.int32, (SUB_ROWS, LANES), 1) < HEAD_DIM
    row8 = lax.broadcasted_iota(jnp.int32, (8, 1), 0)
    g_pre = g_ref[layer:layer + 1, :]
    g_conv = gco_ref[layer:layer + 1, :]

    def sub_tile(t, hb, prev):
        rows = slice(t * SUB_ROWS, (t + 1) * SUB_ROWS)

        def project(j):
            return jnp.dot(hb, w_ref[:, j * CONV_WIDTH:(j + 1) * CONV_WIDTH], preferred_element_type=F32)

        h = project(0)
        u = project(2) * h
        u1 = pltpu.roll(u, 1, 0)
        u2 = pltpu.roll(u, 2, 0)
        u1 = jnp.concatenate([jnp.where(row8 == 0, prev[7:8], u1[:8]), u1[8:]], axis=0)
        u2 = jnp.concatenate(
            [jnp.where(row8 == 0, prev[6:7], jnp.where(row8 == 1, prev[7:8], u2[:8])), u2[8:]], axis=0)
        conv = project(1) * (u2 * wc_ref[0:1] + u1 * wc_ref[1:2] + u * wc_ref[2:3])
        for j in range(CONV_WIDTH // LANES):
            sl = slice(j * LANES, (j + 1) * LANES)
            yc_ref[0, rows, sl] = _pair_group_norm(conv[:, sl], g_conv[:, sl]).astype(BF16)

        def head_groups(ref, val, fill):
            for pair in range(N_HEADS // 2):
                sl = slice(pair * LANES, (pair + 1) * LANES)
                first = slice(2 * pair * LANES, (2 * pair + 1) * LANES)
                second = slice((2 * pair + 1) * LANES, (2 * pair + 2) * LANES)
                ref[0, rows, first] = jnp.where(lo, val[:, sl], fill).astype(BF16)
                ref[0, rows, second] = jnp.where(lo, fill, val[:, sl]).astype(BF16)

        head_groups(q_ref, project(3) * (LOG2E * HEAD_DIM ** -0.5), 0.0)
        k_ref[0, rows, :] = project(4).astype(BF16)
        head_groups(v_ref, project(5), 1.0)
        return u[SUB_ROWS - 8:]

    prev = carry_ref[...]
    for t in range(n_sub):
        hb = _rms(x_ref[0, t * SUB_ROWS:(t + 1) * SUB_ROWS, :], g_pre).astype(BF16)
        prev = sub_tile(t, hb, prev)
    carry_ref[...] = prev


def _inproj(x, layer, g, w, wc, gco):
    b, s, d = x.shape
    tm = TM_PROJ
    row = lambda bi, i: (bi, i, 0)
    narrow = jax.ShapeDtypeStruct((b, s, CONV_WIDTH), BF16)
    wide = jax.ShapeDtypeStruct((b, s, N_HEADS * LANES), BF16)
    narrow_spec = pl.BlockSpec((1, tm, CONV_WIDTH), row)
    wide_spec = pl.BlockSpec((1, tm, N_HEADS * LANES), row)
    return pl.pallas_call(
        functools.partial(_inproj_kernel, layer=layer),
        grid=(b, s // tm),
        in_specs=[
            pl.BlockSpec((1, tm, d), row),
            _whole_spec(g, 2),
            _whole_spec(w, 2),
            _layer_spec(wc, layer, 2),
            _whole_spec(gco, 2),
        ],
        out_specs=[narrow_spec, wide_spec, narrow_spec, wide_spec],
        out_shape=[narrow, wide, narrow, wide],
        scratch_shapes=[pltpu.VMEM((8, CONV_WIDTH), F32)],
        compiler_params=pltpu.CompilerParams(
            dimension_semantics=("arbitrary", "arbitrary"), vmem_limit_bytes=VMEM_LIMIT),
        name="inproj_conv",
    )(x, g, w, wc, gco)


def _table_kernel(f_ref, o_ref):
    qc = lax.broadcasted_iota(jnp.int32, (TQ, NK), 0) // CHUNK
    kc = lax.broadcasted_iota(jnp.int32, (TQ, NK), 1) // CHUNK
    valid = (kc >= qc) & (kc <= qc + N_LEFT_CHUNKS)
    for head in range(N_HEADS):
        f = jnp.broadcast_to(f_ref[head] - f_ref[head][:, 0:1], (TQ, TABLE_WIDTH))
        r = pltpu.roll(f, 0, 1, stride=1, stride_axis=0)[:, :NK]
        o_ref[head] = jnp.where(valid, r * LOG2E, NEG_INF)


def _bias_tables(rel_bias):
    depth, h, _ = rel_bias.shape
    m = jnp.arange(TABLE_WIDTH)
    m = jnp.where(m < NK, m, m - TABLE_WIDTH)
    idx = jnp.clip(LEFT_PAD - m, -REL_CLIP, REL_CLIP) + REL_CLIP
    f = rel_bias.astype(F32)[:, :, idx].reshape(depth, h, 1, TABLE_WIDTH)
    return pl.pallas_call(
        _table_kernel,
        grid=(depth,),
        in_specs=[pl.BlockSpec((None, h, 1, TABLE_WIDTH), lambda i: (i, 0, 0, 0))],
        out_specs=pl.BlockSpec((None, h, TQ, NK), lambda i: (i, 0, 0, 0)),
        out_shape=jax.ShapeDtypeStruct((depth, h, TQ, NK), F32),
        name="bias_tables",
    )(f)


def _table_tile_classes():
    qi = np.arange(TQ)[:, None]
    kj = np.arange(NK)[None, :]
    valid = (kj // CHUNK >= qi // CHUNK) & (kj // CHUNK <= qi // CHUNK + N_LEFT_CHUNKS)
    far = (LEFT_PAD + qi - kj) >= REL_CLIP
    classes = {}
    for qc in range(TQ // CHUNK):
        for ct in range(NK // LANES):
            blk = (slice(qc * CHUNK, (qc + 1) * CHUNK), slice(ct * LANES, (ct + 1) * LANES))
            if not valid[blk].any():
                classes[qc, ct] = "dead"
            elif valid[blk].all() and far[blk].all():
                classes[qc, ct] = "zero"
            else:
                classes[qc, ct] = "add"
    return classes


TILE_CLASS = _table_tile_classes()


def _attn_kernel(q_ref, kp_ref, kc_ref, vp_ref, vc_ref, tab_ref, g_ref, *rest, layer):
    n_cast = len(rest) // 2
    o_ref = rest[n_cast]
    for src, dst in zip(rest[:n_cast], rest[n_cast + 1:]):
        dst[...] = src[...].astype(BF16)

    i = pl.program_id(1)
    lo = lax.broadcasted_iota(jnp.int32, (TQ, LANES), 1) < HEAD_DIM
    nt = (((1,), (1,)), ((), ()))
    g_attn = g_ref[layer:layer + 1, :]

    def attend(first_step):
        def key_parts(j):
            lo_row = (j + 1 - N_KBLK) * TQ
            parts = []
            if lo_row < 0 and not first_step:
                parts.append((kp_ref, vp_ref, LEFT_PAD + lo_row, LEFT_PAD))
            parts.append((kc_ref, vc_ref, max(lo_row, 0), (j + 1) * TQ))
            return parts

        def scores(unit):
            j, head = unit
            pair = slice(head // 2 * LANES, (head // 2 + 1) * LANES)
            hl = slice(head * LANES, (head + 1) * LANES)
            q = q_ref[0, j * TQ:(j + 1) * TQ, hl]
            s = jnp.concatenate(
                [lax.dot_general(q, k[0, r0:r1, pair], nt, preferred_element_type=F32)
                 for k, _, r0, r1 in key_parts(j)], axis=1)
            ct0 = (NK - s.shape[1]) // LANES
            out = []
            for qc in range(TQ // CHUNK):
                rows = slice(qc * CHUNK, (qc + 1) * CHUNK)
                tiles = []
                for ct in range(ct0, NK // LANES):
                    cls = TILE_CLASS[qc, ct]
                    if cls == "dead":
                        continue
                    tile = s[rows, (ct - ct0) * LANES:(ct - ct0 + 1) * LANES]
                    if cls == "add":
                        tile = tile + tab_ref[head, rows, ct * LANES:(ct + 1) * LANES]
                    tiles.append((ct, tile))
                mx = jnp.max(functools.reduce(jnp.maximum, [t for _, t in tiles]), axis=-1, keepdims=True)
                out.append((tiles, mx))
            return ct0, out

        def weighted_values(unit, ct0, chunks):
            j, head = unit
            hl = slice(head * LANES, (head + 1) * LANES)
            p_rows = []
            for tiles, mx in chunks:
                live = dict(tiles)
                p_rows.append(jnp.concatenate(
                    [jnp.exp2(live[ct] - mx).astype(BF16) if ct in live else jnp.zeros((CHUNK, LANES), BF16)
                     for ct in range(ct0, NK // LANES)], axis=1))
            p = jnp.concatenate(p_rows, axis=0)
            acc, col = None, 0
            for _, v, r0, r1 in key_parts(j):
                part = jnp.dot(p[:, col:col + r1 - r0], v[0, r0:r1, hl], preferred_element_type=F32)
                acc = part if acc is None else acc + part
                col += r1 - r0
            return acc

        units = [(j, h) for j in range(Q_BLOCKS) for h in range(N_HEADS)]
        pending = [scores(u) for u in units[:SCORE_LEAD]]
        acc = []
        for n, unit in enumerate(units):
            if n + SCORE_LEAD < len(units):
                pending.append(scores(units[n + SCORE_LEAD]))
            acc.append(weighted_values(unit, *pending.pop(0)))
            j, head = unit
            if head % 2 == 1:
                sl = slice(head // 2 * LANES, (head // 2 + 1) * LANES)
                a0, a1 = acc
                acc = []
                y = jnp.where(lo, a0 / pltpu.roll(a0, HEAD_DIM, 1), a1 / pltpu.roll(a1, HEAD_DIM, 1))
                o_ref[0, j * TQ:(j + 1) * TQ, sl] = _pair_group_norm(y, g_attn[:, sl]).astype(BF16)

    @pl.when(i == 0)
    def _():
        attend(True)

    @pl.when(i > 0)
    def _():
        attend(False)


def _attention(q, k, v, layer, tab, g, casts):
    b, s, w = k.shape
    n_i = s // TQS
    n_steps = b * n_i
    cur = lambda bi, i: (bi, i, 0)
    prev = lambda bi, i: (bi, jnp.maximum(i * (TQS // LEFT_PAD) - 1, 0), 0)
    cast_in, cast_out, cast_shape = [], [], []
    for a, lyr in casts:
        _, r, c = a.shape
        per = 1
        while r * per % (n_steps * BF16_ROW_TILE):
            per *= 2
        rows = r * per // n_steps
        cast_in.append(pl.BlockSpec(
            (None, rows, c), lambda bi, i, lyr=lyr, per=per: (lyr, (bi * n_i + i) // per, 0)))
        cast_out.append(pl.BlockSpec((rows, c), lambda bi, i, per=per: ((bi * n_i + i) // per, 0)))
        cast_shape.append(jax.ShapeDtypeStruct((r, c), BF16))
    out = pl.pallas_call(
        functools.partial(_attn_kernel, layer=layer),
        grid=(b, n_i),
        in_specs=[
            pl.BlockSpec((1, TQS, q.shape[-1]), cur),
            pl.BlockSpec((1, LEFT_PAD, w), prev),
            pl.BlockSpec((1, TQS, w), cur),
            pl.BlockSpec((1, LEFT_PAD, v.shape[-1]), prev),
            pl.BlockSpec((1, TQS, v.shape[-1]), cur),
            _layer_spec(tab, layer, 2, single_buffer=True),
            _whole_spec(g, 2),
        ] + cast_in,
        out_specs=[pl.BlockSpec((1, TQS, w), cur)] + cast_out,
        out_shape=[jax.ShapeDtypeStruct((b, s, w), BF16)] + cast_shape,
        compiler_params=pltpu.CompilerParams(
            dimension_semantics=("arbitrary", "arbitrary"), vmem_limit_bytes=VMEM_LIMIT),
        name="band_attention",
    )(q, k, k, v, v, tab, g, *[a for a, _ in casts])
    return out[0], out[1:]


def _ffn_kernel(x_ref, yc_ref, ya_ref, wo_ref, gpm_ref, gpf_ref, wi_ref, wd_ref, gqf_ref, o_ref, *, layer):
    n_sub = x_ref.shape[0] // SUB_ROWS
    n_chunks = D_FF // FF_CHUNK
    g_post_mix = gpm_ref[layer:layer + 1, :]
    g_pre_ffn = gpf_ref[layer:layer + 1, :]
    g_post_ffn = gqf_ref[layer:layer + 1, :]

    def rows(t):
        return slice(t * SUB_ROWS, (t + 1) * SUB_ROWS)

    def out_proj(t):
        y = jnp.dot(yc_ref[rows(t), :], wo_ref[0:CONV_WIDTH], preferred_element_type=F32)
        return y + jnp.dot(ya_ref[rows(t), :], wo_ref[CONV_WIDTH:], preferred_element_type=F32)

    def mid_norms(t, y):
        x = x_ref[rows(t), :] + _rms(y, g_post_mix)
        return x, _rms(x, g_pre_ffn).astype(BF16)

    def ffn_chunk(hb, c):
        sl = slice(c * FF_CHUNK, (c + 1) * FF_CHUNK)
        sl_up = slice(D_FF + c * FF_CHUNK, D_FF + (c + 1) * FF_CHUNK)
        gate = jnp.dot(hb, wi_ref[:, sl], preferred_element_type=F32)
        up = jnp.dot(hb, wi_ref[:, sl_up], preferred_element_type=F32)
        a = ((gate * jax.nn.sigmoid(gate)) * up).astype(BF16)
        return jnp.dot(a, wd_ref[sl, :], preferred_element_type=F32)

    def finish(t, x, f):
        o_ref[rows(t), :] = x + _rms(f, g_post_ffn)

    ys = [out_proj(t) for t in range(n_sub)]
    x, hb = mid_norms(0, ys[0])
    done = None
    for t in range(n_sub):
        f = None
        nxt = None
        for c in range(n_chunks):
            d = ffn_chunk(hb, c)
            f = d if f is None else f + d
            if c == 0 and done is not None:
                finish(*done)
            if c == n_chunks // 2 and t + 1 < n_sub:
                nxt = mid_norms(t + 1, ys[t + 1])
        done = (t, x, f)
        if nxt is not None:
            x, hb = nxt
    finish(*done)


def _outproj_ffn(x, yc, ya, layer, wo, gpm, gpf, wi, wd, gqf):
    t, d = x.shape
    tm = TM_FFN
    row = lambda i: (i, 0)
    return pl.pallas_call(
        functools.partial(_ffn_kernel, layer=layer),
        grid=(t // tm,),
        in_specs=[
            pl.BlockSpec((tm, d), row),
            pl.BlockSpec((tm, CONV_WIDTH), row),
            pl.BlockSpec((tm, ATTN_WIDTH), row),
            _whole_spec(wo, 1),
            _whole_spec(gpm, 1),
            _whole_spec(gpf, 1),
            _whole_spec(wi, 1),
            _whole_spec(wd, 1),
            _whole_spec(gqf, 1),
        ],
        out_specs=pl.BlockSpec((tm, d), row),
        out_shape=jax.ShapeDtypeStruct((t, d), F32),
        compiler_params=pltpu.CompilerParams(
            dimension_semantics=("arbitrary",), vmem_limit_bytes=VMEM_LIMIT),
        name="outproj_ffn",
    )(x, yc, ya, wo, gpm, gpf, wi, wd, gqf)


def kernel(x, w_in, w_conv, rel_bias, g_conv_out, g_attn_out, w_out, g_pre_mix, g_post_mix,
           g_pre_ffn, g_post_ffn, w_ffn_in, w_ffn_out):
    b, s, d = x.shape
    depth = w_in.shape[0]
    tabs = _bias_tables(rel_bias)
    w_conv_t = jnp.swapaxes(w_conv, 1, 2)
    w_in_b = w_in[0].astype(BF16)
    for l in range(depth):
        yc, q, k, v = _inproj(x, l, g_pre_mix, w_in_b, w_conv_t, g_conv_out)
        casts = [(w_out, l), (w_ffn_in, l), (w_ffn_out, l)]
        if l + 1 < depth:
            casts.append((w_in, l + 1))
        ya, cast = _attention(q, k, v, l, tabs, g_attn_out, casts)
        x = _outproj_ffn(
            x.reshape(b * s, d), yc.reshape(b * s, CONV_WIDTH), ya.reshape(b * s, ATTN_WIDTH), l,
            cast[0], g_post_mix, g_pre_ffn, cast[1], cast[2], g_post_ffn).reshape(b, s, d)
        if l + 1 < depth:
            w_in_b = cast[3]
    return x
```

```python
import functools
import math

import jax
import jax.numpy as jnp
import numpy as np
from jax import lax
from jax.experimental import pallas as pl
from jax.experimental.pallas import tpu as pltpu

D_MODEL = 1024
CHUNK = 64
N_LEFT_CHUNKS = 8
LEFT_PAD = N_LEFT_CHUNKS * CHUNK
CONV_WIDTH = 512
ATTN_WIDTH = 512
HEAD_DIM = 64
N_HEADS = 8
CONV_K = 3
REL_CLIP = 128
D_FF = 2816
EPS = 1e-6
NEG_INF = -1e30
LOG2E = math.log2(math.e)

LANES = 128
TM_PROJ = 1024
TM_FFN = 1024
SUB_ROWS = 512
TQ = 256
N_KBLK = LEFT_PAD // TQ + 1
NK = N_KBLK * TQ
Q_BLOCKS = 4
TQS = Q_BLOCKS * TQ
SCORE_LEAD = 2
BF16_ROW_TILE = 16
TABLE_WIDTH = 1024
FF_CHUNK = 256
VMEM_LIMIT = 56 * 1024 * 1024

F32 = jnp.float32
BF16 = jnp.bfloat16


def _rms(x, g):
    return (x * lax.rsqrt(jnp.mean(x * x, axis=-1, keepdims=True) + EPS)) * g


def _pair_group_norm(y, g):
    lane = lax.broadcasted_iota(jnp.int32, y.shape, 1)
    lo = lane < HEAD_DIM
    sq = y * y
    ms_lo = jnp.sum(jnp.where(lo, sq, 0.0), axis=-1, keepdims=True) * (1.0 / HEAD_DIM)
    ms_hi = jnp.sum(jnp.where(lo, 0.0, sq), axis=-1, keepdims=True) * (1.0 / HEAD_DIM)
    inv = jnp.where(lo, lax.rsqrt(ms_lo + EPS), lax.rsqrt(ms_hi + EPS))
    return (y * inv) * g


def _layer_spec(a, layer, n_grid, single_buffer=False):
    zeros = (0,) * (a.ndim - 1)
    index = {1: lambda i: (layer,) + zeros, 2: lambda bi, i: (layer,) + zeros}[n_grid]
    mode = pl.Buffered(1) if single_buffer else None
    return pl.BlockSpec((None,) + a.shape[1:], index, pipeline_mode=mode)


def _whole_spec(a, n_grid):
    index = {1: lambda i: (0, 0), 2: lambda bi, i: (0, 0)}[n_grid]
    return pl.BlockSpec(a.shape, index, pipeline_mode=pl.Buffered(1))


def _inproj_kernel(x_ref, g_ref, w_ref, wc_ref, gco_ref, yc_ref, q_ref, k_ref, v_ref, carry_ref, *, layer):
    @pl.when(pl.program_id(1) == 0)
    def _():
        carry_ref[...] = jnp.zeros_like(carry_ref)

    n_sub = x_ref.shape[1] // SUB_ROWS
    lo = lax.broadcasted_iota(jnp.int32, (SUB_ROWS, LANES), 1) < HEAD_DIM
    row8 = lax.broadcasted_iota(jnp.int32, (8, 1), 0)
    g_pre = g_ref[layer:layer + 1, :]
    g_conv = gco_ref[layer:layer + 1, :]

    def sub_tile(t, hb, prev):
        rows = slice(t * SUB_ROWS, (t + 1) * SUB_ROWS)

        def project(j):
            return jnp.dot(hb, w_ref[:, j * CONV_WIDTH:(j + 1) * CONV_WIDTH], preferred_element_type=F32)

        h = project(0)
        u = project(2) * h
        u1 = pltpu.roll(u, 1, 0)
        u2 = pltpu.roll(u, 2, 0)
        u1 = jnp.concatenate([jnp.where(row8 == 0, prev[7:8], u1[:8]), u1[8:]], axis=0)
        u2 = jnp.concatenate(
            [jnp.where(row8 == 0, prev[6:7], jnp.where(row8 == 1, prev[7:8], u2[:8])), u2[8:]], axis=0)
        conv = project(1) * (u2 * wc_ref[0:1] + u1 * wc_ref[1:2] + u * wc_ref[2:3])
        for j in range(CONV_WIDTH // LANES):
            sl = slice(j * LANES, (j + 1) * LANES)
            yc_ref[0, rows, sl] = _pair_group_norm(conv[:, sl], g_conv[:, sl]).astype(BF16)

        def head_groups(ref, val, fill):
            for pair in range(N_HEADS // 2):
                sl = slice(pair * LANES, (pair + 1) * LANES)
                first = slice(2 * pair * LANES, (2 * pair + 1) * LANES)
                second = slice((2 * pair + 1) * LANES, (2 * pair + 2) * LANES)
                ref[0, rows, first] = jnp.where(lo, val[:, sl], fill).astype(BF16)
                ref[0, rows, second] = jnp.where(lo, fill, val[:, sl]).astype(BF16)

        head_groups(q_ref, project(3) * (LOG2E * HEAD_DIM ** -0.5), 0.0)
        k_ref[0, rows, :] = project(4).astype(BF16)
        head_groups(v_ref, project(5), 1.0)
        return u[SUB_ROWS - 8:]

    prev = carry_ref[...]
    for t in range(n_sub):
        hb = _rms(x_ref[0, t * SUB_ROWS:(t + 1) * SUB_ROWS, :], g_pre).astype(BF16)
        prev = sub_tile(t, hb, prev)
    carry_ref[...] = prev


def _inproj(x, layer, g, w, wc, gco):
    b, s, d = x.shape
    tm = TM_PROJ
    row = lambda bi, i: (bi, i, 0)
    narrow = jax.ShapeDtypeStruct((b, s, CONV_WIDTH), BF16)
    wide = jax.ShapeDtypeStruct((b, s, N_HEADS * LANES), BF16)
    narrow_spec = pl.BlockSpec((1, tm, CONV_WIDTH), row)
    wide_spec = pl.BlockSpec((1, tm, N_HEADS * LANES), row)
    return pl.pallas_call(
        functools.partial(_inproj_kernel, layer=layer),
        grid=(b, s // tm),
        in_specs=[
            pl.BlockSpec((1, tm, d), row),
            _whole_spec(g, 2),
            _whole_spec(w, 2),
            _layer_spec(wc, layer, 2),
            _whole_spec(gco, 2),
        ],
        out_specs=[narrow_spec, wide_spec, narrow_spec, wide_spec],
        out_shape=[narrow, wide, narrow, wide],
        scratch_shapes=[pltpu.VMEM((8, CONV_WIDTH), F32)],
        compiler_params=pltpu.CompilerParams(
            dimension_semantics=("arbitrary", "arbitrary"), vmem_limit_bytes=VMEM_LIMIT),
        name="inproj_conv",
    )(x, g, w, wc, gco)


def _table_kernel(f_ref, o_ref):
    qc = lax.broadcasted_iota(jnp.int32, (TQ, NK), 0) // CHUNK
    kc = lax.broadcasted_iota(jnp.int32, (TQ, NK), 1) // CHUNK
    valid = (kc >= qc) & (kc <= qc + N_LEFT_CHUNKS)
    for head in range(N_HEADS):
        f = jnp.broadcast_to(f_ref[head] - f_ref[head][:, 0:1], (TQ, TABLE_WIDTH))
        r = pltpu.roll(f, 0, 1, stride=1, stride_axis=0)[:, :NK]
        o_ref[head] = jnp.where(valid, r * LOG2E, NEG_INF)


def _bias_tables(rel_bias):
    depth, h, _ = rel_bias.shape
    m = jnp.arange(TABLE_WIDTH)
    m = jnp.where(m < NK, m, m - TABLE_WIDTH)
    idx = jnp.clip(LEFT_PAD - m, -REL_CLIP, REL_CLIP) + REL_CLIP
    f = rel_bias.astype(F32)[:, :, idx].reshape(depth, h, 1, TABLE_WIDTH)
    return pl.pallas_call(
        _table_kernel,
        grid=(depth,),
        in_specs=[pl.BlockSpec((None, h, 1, TABLE_WIDTH), lambda i: (i, 0, 0, 0))],
        out_specs=pl.BlockSpec((None, h, TQ, NK), lambda i: (i, 0, 0, 0)),
        out_shape=jax.ShapeDtypeStruct((depth, h, TQ, NK), F32),
        name="bias_tables",
    )(f)


def _table_tile_classes():
    qi = np.arange(TQ)[:, None]
    kj = np.arange(NK)[None, :]
    valid = (kj // CHUNK >= qi // CHUNK) & (kj // CHUNK <= qi // CHUNK + N_LEFT_CHUNKS)
    far = (LEFT_PAD + qi - kj) >= REL_CLIP
    classes = {}
    for qc in range(TQ // CHUNK):
        for ct in range(NK // LANES):
            blk = (slice(qc * CHUNK, (qc + 1) * CHUNK), slice(ct * LANES, (ct + 1) * LANES))
            if not valid[blk].any():
                classes[qc, ct] = "dead"
            elif valid[blk].all() and far[blk].all():
                classes[qc, ct] = "zero"
            else:
                classes[qc, ct] = "add"
    return classes


TILE_CLASS = _table_tile_classes()


def _attn_kernel(q_ref, kp_ref, kc_ref, vp_ref, vc_ref, tab_ref, g_ref, *rest, layer):
    n_cast = len(rest) // 2
    o_ref = rest[n_cast]
    for src, dst in zip(rest[:n_cast], rest[n_cast + 1:]):
        dst[...] = src[...].astype(BF16)

    i = pl.program_id(1)
    lo = lax.broadcasted_iota(jnp.int32, (TQ, LANES), 1) < HEAD_DIM
    nt = (((1,), (1,)), ((), ()))
    g_attn = g_ref[layer:layer + 1, :]

    def attend(first_step):
        def key_parts(j):
            lo_row = (j + 1 - N_KBLK) * TQ
            parts = []
            if lo_row < 0 and not first_step:
                parts.append((kp_ref, vp_ref, LEFT_PAD + lo_row, LEFT_PAD))
            parts.append((kc_ref, vc_ref, max(lo_row, 0), (j + 1) * TQ))
            return parts

        def scores(unit):
            j, head = unit
            pair = slice(head // 2 * LANES, (head // 2 + 1) * LANES)
            hl = slice(head * LANES, (head + 1) * LANES)
            q = q_ref[0, j * TQ:(j + 1) * TQ, hl]
            s = jnp.concatenate(
                [lax.dot_general(q, k[0, r0:r1, pair], nt, preferred_element_type=F32)
                 for k, _, r0, r1 in key_parts(j)], axis=1)
            ct0 = (NK - s.shape[1]) // LANES
            out = []
            for qc in range(TQ // CHUNK):
                rows = slice(qc * CHUNK, (qc + 1) * CHUNK)
                tiles = []
                for ct in range(ct0, NK // LANES):
                    cls = TILE_CLASS[qc, ct]
                    if cls == "dead":
                        continue
                    tile = s[rows, (ct - ct0) * LANES:(ct - ct0 + 1) * LANES]
                    if cls == "add":
                        tile = tile + tab_ref[head, rows, ct * LANES:(ct + 1) * LANES]
                    tiles.append((ct, tile))
                mx = jnp.max(functools.reduce(jnp.maximum, [t for _, t in tiles]), axis=-1, keepdims=True)
                out.append((tiles, mx))
            return ct0, out

        def weighted_values(unit, ct0, chunks):
            j, head = unit
            hl = slice(head * LANES, (head + 1) * LANES)
            p_rows = []
            for tiles, mx in chunks:
                live = dict(tiles)
                p_rows.append(jnp.concatenate(
                    [jnp.exp2(live[ct] - mx).astype(BF16) if ct in live else jnp.zeros((CHUNK, LANES), BF16)
                     for ct in range(ct0, NK // LANES)], axis=1))
            p = jnp.concatenate(p_rows, axis=0)
            acc, col = None, 0
            for _, v, r0, r1 in key_parts(j):
                part = jnp.dot(p[:, col:col + r1 - r0], v[0, r0:r1, hl], preferred_element_type=F32)
                acc = part if acc is None else acc + part
                col += r1 - r0
            return acc

        units = [(j, h) for j in range(Q_BLOCKS) for h in range(N_HEADS)]
        pending = [scores(u) for u in units[:SCORE_LEAD]]
        acc = []
        for n, unit in enumerate(units):
            if n + SCORE_LEAD < len(units):
                pending.append(scores(units[n + SCORE_LEAD]))
            acc.append(weighted_values(unit, *pending.pop(0)))
            j, head = unit
            if head % 2 == 1:
                sl = slice(head // 2 * LANES, (head // 2 + 1) * LANES)
                a0, a1 = acc
                acc = []
                y = jnp.where(lo, a0 / pltpu.roll(a0, HEAD_DIM, 1), a1 / pltpu.roll(a1, HEAD_DIM, 1))
                o_ref[0, j * TQ:(j + 1) * TQ, sl] = _pair_group_norm(y, g_attn[:, sl]).astype(BF16)

    @pl.when(i == 0)
    def _():
        attend(True)

    @pl.when(i > 0)
    def _():
        attend(False)


def _attention(q, k, v, layer, tab, g, casts):
    b, s, w = k.shape
    n_i = s // TQS
    n_steps = b * n_i
    cur = lambda bi, i: (bi, i, 0)
    prev = lambda bi, i: (bi, jnp.maximum(i * (TQS // LEFT_PAD) - 1, 0), 0)
    cast_in, cast_out, cast_shape = [], [], []
    for a, lyr in casts:
        _, r, c = a.shape
        per = 1
        while r * per % (n_steps * BF16_ROW_TILE):
            per *= 2
        rows = r * per // n_steps
        cast_in.append(pl.BlockSpec(
            (None, rows, c), lambda bi, i, lyr=lyr, per=per: (lyr, (bi * n_i + i) // per, 0)))
        cast_out.append(pl.BlockSpec((rows, c), lambda bi, i, per=per: ((bi * n_i + i) // per, 0)))
        cast_shape.append(jax.ShapeDtypeStruct((r, c), BF16))
    out = pl.pallas_call(
        functools.partial(_attn_kernel, layer=layer),
        grid=(b, n_i),
        in_specs=[
            pl.BlockSpec((1, TQS, q.shape[-1]), cur),
            pl.BlockSpec((1, LEFT_PAD, w), prev),
            pl.BlockSpec((1, TQS, w), cur),
            pl.BlockSpec((1, LEFT_PAD, v.shape[-1]), prev),
            pl.BlockSpec((1, TQS, v.shape[-1]), cur),
            _layer_spec(tab, layer, 2, single_buffer=True),
            _whole_spec(g, 2),
        ] + cast_in,
        out_specs=[pl.BlockSpec((1, TQS, w), cur)] + cast_out,
        out_shape=[jax.ShapeDtypeStruct((b, s, w), BF16)] + cast_shape,
        compiler_params=pltpu.CompilerParams(
            dimension_semantics=("arbitrary", "arbitrary"), vmem_limit_bytes=VMEM_LIMIT),
        name="band_attention",
    )(q, k, k, v, v, tab, g, *[a for a, _ in casts])
    return out[0], out[1:]


def _ffn_kernel(x_ref, yc_ref, ya_ref, wo_ref, gpm_ref, gpf_ref, wi_ref, wd_ref, gqf_ref, o_ref, *, layer):
    n_sub = x_ref.shape[0] // SUB_ROWS
    n_chunks = D_FF // FF_CHUNK
    g_post_mix = gpm_ref[layer:layer + 1, :]
    g_pre_ffn = gpf_ref[layer:layer + 1, :]
    g_post_ffn = gqf_ref[layer:layer + 1, :]

    def rows(t):
        return slice(t * SUB_ROWS, (t + 1) * SUB_ROWS)

    def out_proj(t):
        y = jnp.dot(yc_ref[rows(t), :], wo_ref[0:CONV_WIDTH], preferred_element_type=F32)
        return y + jnp.dot(ya_ref[rows(t), :], wo_ref[CONV_WIDTH:], preferred_element_type=F32)

    def mid_norms(t, y):
        x = x_ref[rows(t), :] + _rms(y, g_post_mix)
        return x, _rms(x, g_pre_ffn).astype(BF16)

    def act_chunk(hb, c):
        sl = slice(c * FF_CHUNK, (c + 1) * FF_CHUNK)
        sl_up = slice(D_FF + c * FF_CHUNK, D_FF + (c + 1) * FF_CHUNK)
        gate = jnp.dot(hb, wi_ref[:, sl], preferred_element_type=F32)
        up = jnp.dot(hb, wi_ref[:, sl_up], preferred_element_type=F32)
        return ((gate * jax.nn.sigmoid(gate)) * up).astype(BF16)

    def finish(t, x, f):
        o_ref[rows(t), :] = x + _rms(f, g_post_ffn)

    ys = [out_proj(t) for t in range(n_sub)]
    x, hb = mid_norms(0, ys[0])
    done = None
    for t in range(n_sub):
        acts = []
        nxt = None
        for c in range(n_chunks):
            acts.append(act_chunk(hb, c))
            if c == 0 and done is not None:
                finish(*done)
            if c == n_chunks // 2 and t + 1 < n_sub:
                nxt = mid_norms(t + 1, ys[t + 1])
        f = jnp.dot(jnp.concatenate(acts, axis=1), wd_ref[...], preferred_element_type=F32)
        done = (t, x, f)
        if nxt is not None:
            x, hb = nxt
    finish(*done)


def _outproj_ffn(x, yc, ya, layer, wo, gpm, gpf, wi, wd, gqf):
    t, d = x.shape
    tm = TM_FFN
    row = lambda i: (i, 0)
    return pl.pallas_call(
        functools.partial(_ffn_kernel, layer=layer),
        grid=(t // tm,),
        in_specs=[
            pl.BlockSpec((tm, d), row),
            pl.BlockSpec((tm, CONV_WIDTH), row),
            pl.BlockSpec((tm, ATTN_WIDTH), row),
            _whole_spec(wo, 1),
            _whole_spec(gpm, 1),
            _whole_spec(gpf, 1),
            _whole_spec(wi, 1),
            _whole_spec(wd, 1),
            _whole_spec(gqf, 1),
        ],
        out_specs=pl.BlockSpec((tm, d), row),
        out_shape=jax.ShapeDtypeStruct((t, d), F32),
        compiler_params=pltpu.CompilerParams(
            dimension_semantics=("arbitrary",), vmem_limit_bytes=VMEM_LIMIT),
        name="outproj_ffn",
    )(x, yc, ya, wo, gpm, gpf, wi, wd, gqf)


def kernel(x, w_in, w_conv, rel_bias, g_conv_out, g_attn_out, w_out, g_pre_mix, g_post_mix,
           g_pre_ffn, g_post_ffn, w_ffn_in, w_ffn_out):
    b, s, d = x.shape
    depth = w_in.shape[0]
    tabs = _bias_tables(rel_bias)
    w_conv_t = jnp.swapaxes(w_conv, 1, 2)
    w_in_b = w_in[0].astype(BF16)
    for l in range(depth):
        yc, q, k, v = _inproj(x, l, g_pre_mix, w_in_b, w_conv_t, g_conv_out)
        casts = [(w_out, l), (w_ffn_in, l), (w_ffn_out, l)]
        if l + 1 < depth:
            casts.append((w_in, l + 1))
        ya, cast = _attention(q, k, v, l, tabs, g_attn_out, casts)
        x = _outproj_ffn(
            x.reshape(b * s, d), yc.reshape(b * s, CONV_WIDTH), ya.reshape(b * s, ATTN_WIDTH), l,
            cast[0], g_post_mix, g_pre_ffn, cast[1], cast[2], g_post_ffn).reshape(b, s, d)
        if l + 1 < depth:
            w_in_b = cast[3]
    return x
```

```python
import functools
import math

import jax
import jax.numpy as jnp
import numpy as np
from jax import lax
from jax.experimental import pallas as pl
from jax.experimental.pallas import tpu as pltpu

D_MODEL = 1024
CHUNK = 64
N_LEFT_CHUNKS = 8
LEFT_PAD = N_LEFT_CHUNKS * CHUNK
CONV_WIDTH = 512
ATTN_WIDTH = 512
HEAD_DIM = 64
N_HEADS = 8
CONV_K = 3
REL_CLIP = 128
D_FF = 2816
EPS = 1e-6
NEG_INF = -1e30
LOG2E = math.log2(math.e)

LANES = 128
TM_PROJ = 1024
TM_FFN = 1024
SUB_ROWS = 512
TQ = 256
N_KBLK = LEFT_PAD // TQ + 1
NK = N_KBLK * TQ
Q_BLOCKS = 4
TQS = Q_BLOCKS * TQ
SCORE_LEAD = 3
BF16_ROW_TILE = 16
TABLE_WIDTH = 1024
FF_CHUNK = 256
VMEM_LIMIT = 56 * 1024 * 1024

F32 = jnp.float32
BF16 = jnp.bfloat16


def _rms(x, g):
    return (x * lax.rsqrt(jnp.mean(x * x, axis=-1, keepdims=True) + EPS)) * g


def _pair_group_norm(y, g):
    lane = lax.broadcasted_iota(jnp.int32, y.shape, 1)
    lo = lane < HEAD_DIM
    sq = y * y
    ms_lo = jnp.sum(jnp.where(lo, sq, 0.0), axis=-1, keepdims=True) * (1.0 / HEAD_DIM)
    ms_hi = jnp.sum(jnp.where(lo, 0.0, sq), axis=-1, keepdims=True) * (1.0 / HEAD_DIM)
    inv = jnp.where(lo, lax.rsqrt(ms_lo + EPS), lax.rsqrt(ms_hi + EPS))
    return (y * inv) * g


def _layer_spec(a, layer, n_grid, single_buffer=False):
    zeros = (0,) * (a.ndim - 1)
    index = {1: lambda i: (layer,) + zeros, 2: lambda bi, i: (layer,) + zeros}[n_grid]
    mode = pl.Buffered(1) if single_buffer else None
    return pl.BlockSpec((None,) + a.shape[1:], index, pipeline_mode=mode)


def _whole_spec(a, n_grid):
    index = {1: lambda i: (0, 0), 2: lambda bi, i: (0, 0)}[n_grid]
    return pl.BlockSpec(a.shape, index, pipeline_mode=pl.Buffered(1))


def _inproj_kernel(x_ref, g_ref, w_ref, wc_ref, gco_ref, yc_ref, q_ref, k_ref, v_ref, carry_ref, *, layer):
    @pl.when(pl.program_id(1) == 0)
    def _():
        carry_ref[...] = jnp.zeros_like(carry_ref)

    n_sub = x_ref.shape[1] // SUB_ROWS
    lo = lax.broadcasted_iota(jnp.int32, (SUB_ROWS, LANES), 1) < HEAD_DIM
    row8 = lax.broadcasted_iota(jnp.int32, (8, 1), 0)
    g_pre = g_ref[layer:layer + 1, :]
    g_conv = gco_ref[layer:layer + 1, :]

    def sub_tile(t, hb, prev):
        rows = slice(t * SUB_ROWS, (t + 1) * SUB_ROWS)

        def project(j):
            return jnp.dot(hb, w_ref[:, j * CONV_WIDTH:(j + 1) * CONV_WIDTH], preferred_element_type=F32)

        h = project(0)
        u = project(2) * h
        u1 = pltpu.roll(u, 1, 0)
        u2 = pltpu.roll(u, 2, 0)
        u1 = jnp.concatenate([jnp.where(row8 == 0, prev[7:8], u1[:8]), u1[8:]], axis=0)
        u2 = jnp.concatenate(
            [jnp.where(row8 == 0, prev[6:7], jnp.where(row8 == 1, prev[7:8], u2[:8])), u2[8:]], axis=0)
        conv = project(1) * (u2 * wc_ref[0:1] + u1 * wc_ref[1:2] + u * wc_ref[2:3])
        for j in range(CONV_WIDTH // LANES):
            sl = slice(j * LANES, (j + 1) * LANES)
            yc_ref[0, rows, sl] = _pair_group_norm(conv[:, sl], g_conv[:, sl]).astype(BF16)

        def head_groups(ref, val, fill):
            for pair in range(N_HEADS // 2):
                sl = slice(pair * LANES, (pair + 1) * LANES)
                first = slice(2 * pair * LANES, (2 * pair + 1) * LANES)
                second = slice((2 * pair + 1) * LANES, (2 * pair + 2) * LANES)
                ref[0, rows, first] = jnp.where(lo, val[:, sl], fill).astype(BF16)
                ref[0, rows, second] = jnp.where(lo, fill, val[:, sl]).astype(BF16)

        head_groups(q_ref, project(3) * (LOG2E * HEAD_DIM ** -0.5), 0.0)
        k_ref[0, rows, :] = project(4).astype(BF16)
        head_groups(v_ref, project(5), 1.0)
        return u[SUB_ROWS - 8:]

    prev = carry_ref[...]
    for t in range(n_sub):
        hb = _rms(x_ref[0, t * SUB_ROWS:(t + 1) * SUB_ROWS, :], g_pre).astype(BF16)
        prev = sub_tile(t, hb, prev)
    carry_ref[...] = prev


def _inproj(x, layer, g, w, wc, gco):
    b, s, d = x.shape
    tm = TM_PROJ
    row = lambda bi, i: (bi, i, 0)
    narrow = jax.ShapeDtypeStruct((b, s, CONV_WIDTH), BF16)
    wide = jax.ShapeDtypeStruct((b, s, N_HEADS * LANES), BF16)
    narrow_spec = pl.BlockSpec((1, tm, CONV_WIDTH), row)
    wide_spec = pl.BlockSpec((1, tm, N_HEADS * LANES), row)
    return pl.pallas_call(
        functools.partial(_inproj_kernel, layer=layer),
        grid=(b, s // tm),
        in_specs=[
            pl.BlockSpec((1, tm, d), row),
            _whole_spec(g, 2),
            _whole_spec(w, 2),
            _layer_spec(wc, layer, 2),
            _whole_spec(gco, 2),
        ],
        out_specs=[narrow_spec, wide_spec, narrow_spec, wide_spec],
        out_shape=[narrow, wide, narrow, wide],
        scratch_shapes=[pltpu.VMEM((8, CONV_WIDTH), F32)],
        compiler_params=pltpu.CompilerParams(
            dimension_semantics=("arbitrary", "arbitrary"), vmem_limit_bytes=VMEM_LIMIT),
        name="inproj_conv",
    )(x, g, w, wc, gco)


def _table_kernel(f_ref, o_ref):
    qc = lax.broadcasted_iota(jnp.int32, (TQ, NK), 0) // CHUNK
    kc = lax.broadcasted_iota(jnp.int32, (TQ, NK), 1) // CHUNK
    valid = (kc >= qc) & (kc <= qc + N_LEFT_CHUNKS)
    for head in range(N_HEADS):
        f = jnp.broadcast_to(f_ref[head] - f_ref[head][:, 0:1], (TQ, TABLE_WIDTH))
        r = pltpu.roll(f, 0, 1, stride=1, stride_axis=0)[:, :NK]
        o_ref[head] = jnp.where(valid, r * LOG2E, NEG_INF)


def _bias_tables(rel_bias):
    depth, h, _ = rel_bias.shape
    m = jnp.arange(TABLE_WIDTH)
    m = jnp.where(m < NK, m, m - TABLE_WIDTH)
    idx = jnp.clip(LEFT_PAD - m, -REL_CLIP, REL_CLIP) + REL_CLIP
    f = rel_bias.astype(F32)[:, :, idx].reshape(depth, h, 1, TABLE_WIDTH)
    return pl.pallas_call(
        _table_kernel,
        grid=(depth,),
        in_specs=[pl.BlockSpec((None, h, 1, TABLE_WIDTH), lambda i: (i, 0, 0, 0))],
        out_specs=pl.BlockSpec((None, h, TQ, NK), lambda i: (i, 0, 0, 0)),
        out_shape=jax.ShapeDtypeStruct((depth, h, TQ, NK), F32),
        name="bias_tables",
    )(f)


def _table_tile_classes():
    qi = np.arange(TQ)[:, None]
    kj = np.arange(NK)[None, :]
    valid = (kj // CHUNK >= qi // CHUNK) & (kj // CHUNK <= qi // CHUNK + N_LEFT_CHUNKS)
    far = (LEFT_PAD + qi - kj) >= REL_CLIP
    classes = {}
    for qc in range(TQ // CHUNK):
        for ct in range(NK // LANES):
            blk = (slice(qc * CHUNK, (qc + 1) * CHUNK), slice(ct * LANES, (ct + 1) * LANES))
            if not valid[blk].any():
                classes[qc, ct] = "dead"
            elif valid[blk].all() and far[blk].all():
                classes[qc, ct] = "zero"
            else:
                classes[qc, ct] = "add"
    return classes


TILE_CLASS = _table_tile_classes()


def _attn_kernel(q_ref, kp_ref, kc_ref, vp_ref, vc_ref, tab_ref, g_ref, *rest, layer):
    n_cast = len(rest) // 2
    o_ref = rest[n_cast]
    for src, dst in zip(rest[:n_cast], rest[n_cast + 1:]):
        dst[...] = src[...].astype(BF16)

    i = pl.program_id(1)
    lo = lax.broadcasted_iota(jnp.int32, (TQ, LANES), 1) < HEAD_DIM
    nt = (((1,), (1,)), ((), ()))
    g_attn = g_ref[layer:layer + 1, :]

    def attend(first_step):
        def key_parts(j):
            lo_row = (j + 1 - N_KBLK) * TQ
            parts = []
            if lo_row < 0 and not first_step:
                parts.append((kp_ref, vp_ref, LEFT_PAD + lo_row, LEFT_PAD))
            parts.append((kc_ref, vc_ref, max(lo_row, 0), (j + 1) * TQ))
            return parts

        def scores(unit):
            j, head = unit
            pair = slice(head // 2 * LANES, (head // 2 + 1) * LANES)
            hl = slice(head * LANES, (head + 1) * LANES)
            q = q_ref[0, j * TQ:(j + 1) * TQ, hl]
            s = jnp.concatenate(
                [lax.dot_general(q, k[0, r0:r1, pair], nt, preferred_element_type=F32)
                 for k, _, r0, r1 in key_parts(j)], axis=1)
            ct0 = (NK - s.shape[1]) // LANES
            out = []
            for qc in range(TQ // CHUNK):
                rows = slice(qc * CHUNK, (qc + 1) * CHUNK)
                tiles = []
                for ct in range(ct0, NK // LANES):
                    cls = TILE_CLASS[qc, ct]
                    if cls == "dead":
                        continue
                    tile = s[rows, (ct - ct0) * LANES:(ct - ct0 + 1) * LANES]
                    if cls == "add":
                        tile = tile + tab_ref[head, rows, ct * LANES:(ct + 1) * LANES]
                    tiles.append((ct, tile))
                mx = jnp.max(functools.reduce(jnp.maximum, [t for _, t in tiles]), axis=-1, keepdims=True)
                out.append((tiles, mx))
            return ct0, out

        def weighted_values(unit, ct0, chunks):
            j, head = unit
            hl = slice(head * LANES, (head + 1) * LANES)
            p_rows = []
            for tiles, mx in chunks:
                live = dict(tiles)
                p_rows.append(jnp.concatenate(
                    [jnp.exp2(live[ct] - mx).astype(BF16) if ct in live else jnp.zeros((CHUNK, LANES), BF16)
                     for ct in range(ct0, NK // LANES)], axis=1))
            p = jnp.concatenate(p_rows, axis=0)
            acc, col = None, 0
            for _, v, r0, r1 in key_parts(j):
                part = jnp.dot(p[:, col:col + r1 - r0], v[0, r0:r1, hl], preferred_element_type=F32)
                acc = part if acc is None else acc + part
                col += r1 - r0
            return acc

        units = [(j, h) for j in range(Q_BLOCKS) for h in range(N_HEADS)]
        pending = [scores(u) for u in units[:SCORE_LEAD]]
        acc = []
        for n, unit in enumerate(units):
            if n + SCORE_LEAD < len(units):
                pending.append(scores(units[n + SCORE_LEAD]))
            acc.append(weighted_values(unit, *pending.pop(0)))
            j, head = unit
            if head % 2 == 1:
                sl = slice(head // 2 * LANES, (head // 2 + 1) * LANES)
                a0, a1 = acc
                acc = []
                y = jnp.where(lo, a0 / pltpu.roll(a0, HEAD_DIM, 1), a1 / pltpu.roll(a1, HEAD_DIM, 1))
                o_ref[0, j * TQ:(j + 1) * TQ, sl] = _pair_group_norm(y, g_attn[:, sl]).astype(BF16)

    @pl.when(i == 0)
    def _():
        attend(True)

    @pl.when(i > 0)
    def _():
        attend(False)


def _attention(q, k, v, layer, tab, g, casts):
    b, s, w = k.shape
    n_i = s // TQS
    n_steps = b * n_i
    cur = lambda bi, i: (bi, i, 0)
    prev = lambda bi, i: (bi, jnp.maximum(i * (TQS // LEFT_PAD) - 1, 0), 0)
    cast_in, cast_out, cast_shape = [], [], []
    for a, lyr in casts:
        _, r, c = a.shape
        per = 1
        while r * per % (n_steps * BF16_ROW_TILE):
            per *= 2
        rows = r * per // n_steps
        cast_in.append(pl.BlockSpec(
            (None, rows, c), lambda bi, i, lyr=lyr, per=per: (lyr, (bi * n_i + i) // per, 0)))
        cast_out.append(pl.BlockSpec((rows, c), lambda bi, i, per=per: ((bi * n_i + i) // per, 0)))
        cast_shape.append(jax.ShapeDtypeStruct((r, c), BF16))
    out = pl.pallas_call(
        functools.partial(_attn_kernel, layer=layer),
        grid=(b, n_i),
        in_specs=[
            pl.BlockSpec((1, TQS, q.shape[-1]), cur),
            pl.BlockSpec((1, LEFT_PAD, w), prev),
            pl.BlockSpec((1, TQS, w), cur),
            pl.BlockSpec((1, LEFT_PAD, v.shape[-1]), prev),
            pl.BlockSpec((1, TQS, v.shape[-1]), cur),
            _layer_spec(tab, layer, 2, single_buffer=True),
            _whole_spec(g, 2),
        ] + cast_in,
        out_specs=[pl.BlockSpec((1, TQS, w), cur)] + cast_out,
        out_shape=[jax.ShapeDtypeStruct((b, s, w), BF16)] + cast_shape,
        compiler_params=pltpu.CompilerParams(
            dimension_semantics=("arbitrary", "arbitrary"), vmem_limit_bytes=VMEM_LIMIT),
        name="band_attention",
    )(q, k, k, v, v, tab, g, *[a for a, _ in casts])
    return out[0], out[1:]


def _ffn_kernel(x_ref, yc_ref, ya_ref, wo_ref, gpm_ref, gpf_ref, wi_ref, wd_ref, gqf_ref, o_ref, *, layer):
    n_sub = x_ref.shape[0] // SUB_ROWS
    n_chunks = D_FF // FF_CHUNK
    g_post_mix = gpm_ref[layer:layer + 1, :]
    g_pre_ffn = gpf_ref[layer:layer + 1, :]
    g_post_ffn = gqf_ref[layer:layer + 1, :]

    def rows(t):
        return slice(t * SUB_ROWS, (t + 1) * SUB_ROWS)

    def out_proj(t):
        y = jnp.dot(yc_ref[rows(t), :], wo_ref[0:CONV_WIDTH], preferred_element_type=F32)
        return y + jnp.dot(ya_ref[rows(t), :], wo_ref[CONV_WIDTH:], preferred_element_type=F32)

    def mid_norms(t, y):
        x = x_ref[rows(t), :] + _rms(y, g_post_mix)
        return x, _rms(x, g_pre_ffn).astype(BF16)

    def ffn_chunk(hb, c):
        sl = slice(c * FF_CHUNK, (c + 1) * FF_CHUNK)
        sl_up = slice(D_FF + c * FF_CHUNK, D_FF + (c + 1) * FF_CHUNK)
        gate = jnp.dot(hb, wi_ref[:, sl], preferred_element_type=F32)
        up = jnp.dot(hb, wi_ref[:, sl_up], preferred_element_type=F32)
        a = ((gate * jax.nn.sigmoid(gate)) * up).astype(BF16)
        return jnp.dot(a, wd_ref[sl, :], preferred_element_type=F32)

    def finish(t, x, f):
        o_ref[rows(t), :] = x + _rms(f, g_post_ffn)

    ys = [out_proj(t) for t in range(n_sub)]
    x, hb = mid_norms(0, ys[0])
    done = None
    for t in range(n_sub):
        f = None
        nxt = None
        for c in range(n_chunks):
            d = ffn_chunk(hb, c)
            f = d if f is None else f + d
            if c == 0 and done is not None:
                finish(*done)
            if c == n_chunks // 2 and t + 1 < n_sub:
                nxt = mid_norms(t + 1, ys[t + 1])
        done = (t, x, f)
        if nxt is not None:
            x, hb = nxt
    finish(*done)


def _outproj_ffn(x, yc, ya, layer, wo, gpm, gpf, wi, wd, gqf):
    t, d = x.shape
    tm = TM_FFN
    row = lambda i: (i, 0)
    return pl.pallas_call(
        functools.partial(_ffn_kernel, layer=layer),
        grid=(t // tm,),
        in_specs=[
            pl.BlockSpec((tm, d), row),
            pl.BlockSpec((tm, CONV_WIDTH), row),
            pl.BlockSpec((tm, ATTN_WIDTH), row),
            _whole_spec(wo, 1),
            _whole_spec(gpm, 1),
            _whole_spec(gpf, 1),
            _whole_spec(wi, 1),
            _whole_spec(wd, 1),
            _whole_spec(gqf, 1),
        ],
        out_specs=pl.BlockSpec((tm, d), row),
        out_shape=jax.ShapeDtypeStruct((t, d), F32),
        compiler_params=pltpu.CompilerParams(
            dimension_semantics=("arbitrary",), vmem_limit_bytes=VMEM_LIMIT),
        name="outproj_ffn",
    )(x, yc, ya, wo, gpm, gpf, wi, wd, gqf)


def kernel(x, w_in, w_conv, rel_bias, g_conv_out, g_attn_out, w_out, g_pre_mix, g_post_mix,
           g_pre_ffn, g_post_ffn, w_ffn_in, w_ffn_out):
    b, s, d = x.shape
    depth = w_in.shape[0]
    tabs = _bias_tables(rel_bias)
    w_conv_t = jnp.swapaxes(w_conv, 1, 2)
    w_in_b = w_in[0].astype(BF16)
    for l in range(depth):
        yc, q, k, v = _inproj(x, l, g_pre_mix, w_in_b, w_conv_t, g_conv_out)
        casts = [(w_out, l), (w_ffn_in, l), (w_ffn_out, l)]
        if l + 1 < depth:
            casts.append((w_in, l + 1))
        ya, cast = _attention(q, k, v, l, tabs, g_attn_out, casts)
        x = _outproj_ffn(
            x.reshape(b * s, d), yc.reshape(b * s, CONV_WIDTH), ya.reshape(b * s, ATTN_WIDTH), l,
            cast[0], g_post_mix, g_pre_ffn, cast[1], cast[2], g_post_ffn).reshape(b, s, d)
        if l + 1 < depth:
            w_in_b = cast[3]
    return x
```

```python
import functools
import math

import jax
import jax.numpy as jnp
import numpy as np
from jax import lax
from jax.experimental import pallas as pl
from jax.experimental.pallas import tpu as pltpu

D_MODEL = 1024
CHUNK = 64
N_LEFT_CHUNKS = 8
LEFT_PAD = N_LEFT_CHUNKS * CHUNK
CONV_WIDTH = 512
ATTN_WIDTH = 512
HEAD_DIM = 64
N_HEADS = 8
CONV_K = 3
REL_CLIP = 128
D_FF = 2816
EPS = 1e-6
NEG_INF = -1e30
LOG2E = math.log2(math.e)

LANES = 128
TM_PROJ = 1024
TM_FFN = 1024
SUB_ROWS = 512
TQ = 256
N_KBLK = LEFT_PAD // TQ + 1
NK = N_KBLK * TQ
Q_BLOCKS = 4
TQS = Q_BLOCKS * TQ
SCORE_LEAD = 3
BF16_ROW_TILE = 16
TABLE_WIDTH = 1024
FF_CHUNK = 256
VMEM_LIMIT = 56 * 1024 * 1024

F32 = jnp.float32
BF16 = jnp.bfloat16


def _rms(x, g):
    return (x * lax.rsqrt(jnp.mean(x * x, axis=-1, keepdims=True) + EPS)) * g


def _pair_group_norm(y, g):
    lane = lax.broadcasted_iota(jnp.int32, y.shape, 1)
    lo = lane < HEAD_DIM
    sq = y * y
    ms_lo = jnp.sum(jnp.where(lo, sq, 0.0), axis=-1, keepdims=True) * (1.0 / HEAD_DIM)
    ms_hi = jnp.sum(jnp.where(lo, 0.0, sq), axis=-1, keepdims=True) * (1.0 / HEAD_DIM)
    inv = jnp.where(lo, lax.rsqrt(ms_lo + EPS), lax.rsqrt(ms_hi + EPS))
    return (y * inv) * g


def _layer_spec(a, layer, n_grid, single_buffer=False):
    zeros = (0,) * (a.ndim - 1)
    index = {1: lambda i: (layer,) + zeros, 2: lambda bi, i: (layer,) + zeros}[n_grid]
    mode = pl.Buffered(1) if single_buffer else None
    return pl.BlockSpec((None,) + a.shape[1:], index, pipeline_mode=mode)


def _whole_spec(a, n_grid):
    index = {1: lambda i: (0, 0), 2: lambda bi, i: (0, 0)}[n_grid]
    return pl.BlockSpec(a.shape, index, pipeline_mode=pl.Buffered(1))


def _cast_specs(casts, n_i, n_steps):
    cast_in, cast_out, cast_shape = [], [], []
    for a, lyr in casts:
        _, r, c = a.shape
        per = 1
        while r * per % (n_steps * BF16_ROW_TILE):
            per *= 2
        rows = r * per // n_steps
        cast_in.append(pl.BlockSpec(
            (None, rows, c), lambda bi, i, lyr=lyr, per=per: (lyr, (bi * n_i + i) // per, 0)))
        cast_out.append(pl.BlockSpec((rows, c), lambda bi, i, per=per: ((bi * n_i + i) // per, 0)))
        cast_shape.append(jax.ShapeDtypeStruct((r, c), BF16))
    return cast_in, cast_out, cast_shape


def _inproj_kernel(x_ref, g_ref, w_ref, wc_ref, gco_ref, *rest, layer):
    n_cast = (len(rest) - 5) // 2
    yc_ref, q_ref, k_ref, v_ref = rest[n_cast:n_cast + 4]
    carry_ref = rest[-1]
    for src, dst in zip(rest[:n_cast], rest[n_cast + 4:-1]):
        dst[...] = src[...].astype(BF16)

    @pl.when(pl.program_id(1) == 0)
    def _():
        carry_ref[...] = jnp.zeros_like(carry_ref)

    n_sub = x_ref.shape[1] // SUB_ROWS
    lo = lax.broadcasted_iota(jnp.int32, (SUB_ROWS, LANES), 1) < HEAD_DIM
    row8 = lax.broadcasted_iota(jnp.int32, (8, 1), 0)
    g_pre = g_ref[layer:layer + 1, :]
    g_conv = gco_ref[layer:layer + 1, :]

    def sub_tile(t, hb, prev):
        rows = slice(t * SUB_ROWS, (t + 1) * SUB_ROWS)

        def project(j):
            return jnp.dot(hb, w_ref[:, j * CONV_WIDTH:(j + 1) * CONV_WIDTH], preferred_element_type=F32)

        h = project(0)
        u = project(2) * h
        u1 = pltpu.roll(u, 1, 0)
        u2 = pltpu.roll(u, 2, 0)
        u1 = jnp.concatenate([jnp.where(row8 == 0, prev[7:8], u1[:8]), u1[8:]], axis=0)
        u2 = jnp.concatenate(
            [jnp.where(row8 == 0, prev[6:7], jnp.where(row8 == 1, prev[7:8], u2[:8])), u2[8:]], axis=0)
        conv = project(1) * (u2 * wc_ref[0:1] + u1 * wc_ref[1:2] + u * wc_ref[2:3])
        for j in range(CONV_WIDTH // LANES):
            sl = slice(j * LANES, (j + 1) * LANES)
            yc_ref[0, rows, sl] = _pair_group_norm(conv[:, sl], g_conv[:, sl]).astype(BF16)

        def head_groups(ref, val, fill):
            for pair in range(N_HEADS // 2):
                sl = slice(pair * LANES, (pair + 1) * LANES)
                first = slice(2 * pair * LANES, (2 * pair + 1) * LANES)
                second = slice((2 * pair + 1) * LANES, (2 * pair + 2) * LANES)
                ref[0, rows, first] = jnp.where(lo, val[:, sl], fill).astype(BF16)
                ref[0, rows, second] = jnp.where(lo, fill, val[:, sl]).astype(BF16)

        head_groups(q_ref, project(3) * (LOG2E * HEAD_DIM ** -0.5), 0.0)
        k_ref[0, rows, :] = project(4).astype(BF16)
        head_groups(v_ref, project(5), 1.0)
        return u[SUB_ROWS - 8:]

    prev = carry_ref[...]
    for t in range(n_sub):
        hb = _rms(x_ref[0, t * SUB_ROWS:(t + 1) * SUB_ROWS, :], g_pre).astype(BF16)
        prev = sub_tile(t, hb, prev)
    carry_ref[...] = prev


def _inproj(x, layer, g, w, wc, gco, casts):
    b, s, d = x.shape
    tm = TM_PROJ
    cast_in, cast_out, cast_shape = _cast_specs(casts, s // tm, b * (s // tm))
    row = lambda bi, i: (bi, i, 0)
    narrow = jax.ShapeDtypeStruct((b, s, CONV_WIDTH), BF16)
    wide = jax.ShapeDtypeStruct((b, s, N_HEADS * LANES), BF16)
    narrow_spec = pl.BlockSpec((1, tm, CONV_WIDTH), row)
    wide_spec = pl.BlockSpec((1, tm, N_HEADS * LANES), row)
    out = pl.pallas_call(
        functools.partial(_inproj_kernel, layer=layer),
        grid=(b, s // tm),
        in_specs=[
            pl.BlockSpec((1, tm, d), row),
            _whole_spec(g, 2),
            _whole_spec(w, 2),
            _layer_spec(wc, layer, 2),
            _whole_spec(gco, 2),
        ] + cast_in,
        out_specs=[narrow_spec, wide_spec, narrow_spec, wide_spec] + cast_out,
        out_shape=[narrow, wide, narrow, wide] + cast_shape,
        scratch_shapes=[pltpu.VMEM((8, CONV_WIDTH), F32)],
        compiler_params=pltpu.CompilerParams(
            dimension_semantics=("arbitrary", "arbitrary"), vmem_limit_bytes=VMEM_LIMIT),
        name="inproj_conv",
    )(x, g, w, wc, gco, *[a for a, _ in casts])
    return out[:4], out[4:]


def _table_kernel(f_ref, o_ref):
    qc = lax.broadcasted_iota(jnp.int32, (TQ, NK), 0) // CHUNK
    kc = lax.broadcasted_iota(jnp.int32, (TQ, NK), 1) // CHUNK
    valid = (kc >= qc) & (kc <= qc + N_LEFT_CHUNKS)
    for head in range(N_HEADS):
        f = jnp.broadcast_to(f_ref[head] - f_ref[head][:, 0:1], (TQ, TABLE_WIDTH))
        r = pltpu.roll(f, 0, 1, stride=1, stride_axis=0)[:, :NK]
        o_ref[head] = jnp.where(valid, r * LOG2E, NEG_INF)


def _bias_tables(rel_bias):
    depth, h, _ = rel_bias.shape
    m = jnp.arange(TABLE_WIDTH)
    m = jnp.where(m < NK, m, m - TABLE_WIDTH)
    idx = jnp.clip(LEFT_PAD - m, -REL_CLIP, REL_CLIP) + REL_CLIP
    f = rel_bias.astype(F32)[:, :, idx].reshape(depth, h, 1, TABLE_WIDTH)
    return pl.pallas_call(
        _table_kernel,
        grid=(depth,),
        in_specs=[pl.BlockSpec((None, h, 1, TABLE_WIDTH), lambda i: (i, 0, 0, 0))],
        out_specs=pl.BlockSpec((None, h, TQ, NK), lambda i: (i, 0, 0, 0)),
        out_shape=jax.ShapeDtypeStruct((depth, h, TQ, NK), F32),
        name="bias_tables",
    )(f)


def _table_tile_classes():
    qi = np.arange(TQ)[:, None]
    kj = np.arange(NK)[None, :]
    valid = (kj // CHUNK >= qi // CHUNK) & (kj // CHUNK <= qi // CHUNK + N_LEFT_CHUNKS)
    far = (LEFT_PAD + qi - kj) >= REL_CLIP
    classes = {}
    for qc in range(TQ // CHUNK):
        for ct in range(NK // LANES):
            blk = (slice(qc * CHUNK, (qc + 1) * CHUNK), slice(ct * LANES, (ct + 1) * LANES))
            if not valid[blk].any():
                classes[qc, ct] = "dead"
            elif valid[blk].all() and far[blk].all():
                classes[qc, ct] = "zero"
            else:
                classes[qc, ct] = "add"
    return classes


TILE_CLASS = _table_tile_classes()


def _attn_kernel(q_ref, kp_ref, kc_ref, vp_ref, vc_ref, tab_ref, g_ref, o_ref, *, layer):
    i = pl.program_id(1)
    lo = lax.broadcasted_iota(jnp.int32, (TQ, LANES), 1) < HEAD_DIM
    nt = (((1,), (1,)), ((), ()))
    g_attn = g_ref[layer:layer + 1, :]

    def attend(first_step):
        def key_parts(j):
            lo_row = (j + 1 - N_KBLK) * TQ
            parts = []
            if lo_row < 0 and not first_step:
                parts.append((kp_ref, vp_ref, LEFT_PAD + lo_row, LEFT_PAD))
            parts.append((kc_ref, vc_ref, max(lo_row, 0), (j + 1) * TQ))
            return parts

        def scores(unit):
            j, head = unit
            pair = slice(head // 2 * LANES, (head // 2 + 1) * LANES)
            hl = slice(head * LANES, (head + 1) * LANES)
            q = q_ref[0, j * TQ:(j + 1) * TQ, hl]
            s = jnp.concatenate(
                [lax.dot_general(q, k[0, r0:r1, pair], nt, preferred_element_type=F32)
                 for k, _, r0, r1 in key_parts(j)], axis=1)
            ct0 = (NK - s.shape[1]) // LANES
            out = []
            for qc in range(TQ // CHUNK):
                rows = slice(qc * CHUNK, (qc + 1) * CHUNK)
                tiles = []
                for ct in range(ct0, NK // LANES):
                    cls = TILE_CLASS[qc, ct]
                    if cls == "dead":
                        continue
                    tile = s[rows, (ct - ct0) * LANES:(ct - ct0 + 1) * LANES]
                    if cls == "add":
                        tile = tile + tab_ref[head, rows, ct * LANES:(ct + 1) * LANES]
                    tiles.append((ct, tile))
                mx = jnp.max(functools.reduce(jnp.maximum, [t for _, t in tiles]), axis=-1, keepdims=True)
                out.append((tiles, mx))
            return ct0, out

        def weighted_values(unit, ct0, chunks):
            j, head = unit
            hl = slice(head * LANES, (head + 1) * LANES)
            p_rows = []
            for tiles, mx in chunks:
                live = dict(tiles)
                p_rows.append(jnp.concatenate(
                    [jnp.exp2(live[ct] - mx).astype(BF16) if ct in live else jnp.zeros((CHUNK, LANES), BF16)
                     for ct in range(ct0, NK // LANES)], axis=1))
            p = jnp.concatenate(p_rows, axis=0)
            acc, col = None, 0
            for _, v, r0, r1 in key_parts(j):
                part = jnp.dot(p[:, col:col + r1 - r0], v[0, r0:r1, hl], preferred_element_type=F32)
                acc = part if acc is None else acc + part
                col += r1 - r0
            return acc

        units = [(j, h) for j in range(Q_BLOCKS) for h in range(N_HEADS)]
        pending = [scores(u) for u in units[:SCORE_LEAD]]
        acc = []
        for n, unit in enumerate(units):
            if n + SCORE_LEAD < len(units):
                pending.append(scores(units[n + SCORE_LEAD]))
            acc.append(weighted_values(unit, *pending.pop(0)))
            j, head = unit
            if head % 2 == 1:
                sl = slice(head // 2 * LANES, (head // 2 + 1) * LANES)
                a0, a1 = acc
                acc = []
                y = jnp.where(lo, a0 / pltpu.roll(a0, HEAD_DIM, 1), a1 / pltpu.roll(a1, HEAD_DIM, 1))
                o_ref[0, j * TQ:(j + 1) * TQ, sl] = _pair_group_norm(y, g_attn[:, sl]).astype(BF16)

    @pl.when(i == 0)
    def _():
        attend(True)

    @pl.when(i > 0)
    def _():
        attend(False)


def _attention(q, k, v, layer, tab, g):
    b, s, w = k.shape
    cur = lambda bi, i: (bi, i, 0)
    prev = lambda bi, i: (bi, jnp.maximum(i * (TQS // LEFT_PAD) - 1, 0), 0)
    return pl.pallas_call(
        functools.partial(_attn_kernel, layer=layer),
        grid=(b, s // TQS),
        in_specs=[
            pl.BlockSpec((1, TQS, q.shape[-1]), cur),
            pl.BlockSpec((1, LEFT_PAD, w), prev),
            pl.BlockSpec((1, TQS, w), cur),
            pl.BlockSpec((1, LEFT_PAD, v.shape[-1]), prev),
            pl.BlockSpec((1, TQS, v.shape[-1]), cur),
            _layer_spec(tab, layer, 2, single_buffer=True),
            _whole_spec(g, 2),
        ],
        out_specs=pl.BlockSpec((1, TQS, w), cur),
        out_shape=jax.ShapeDtypeStruct((b, s, w), BF16),
        compiler_params=pltpu.CompilerParams(
            dimension_semantics=("arbitrary", "arbitrary"), vmem_limit_bytes=VMEM_LIMIT),
        name="band_attention",
    )(q, k, k, v, v, tab, g)


def _ffn_kernel(x_ref, yc_ref, ya_ref, wo_ref, gpm_ref, gpf_ref, wi_ref, wd_ref, gqf_ref, o_ref, *, layer):
    n_sub = x_ref.shape[0] // SUB_ROWS
    n_chunks = D_FF // FF_CHUNK
    g_post_mix = gpm_ref[layer:layer + 1, :]
    g_pre_ffn = gpf_ref[layer:layer + 1, :]
    g_post_ffn = gqf_ref[layer:layer + 1, :]

    def rows(t):
        return slice(t * SUB_ROWS, (t + 1) * SUB_ROWS)

    def out_proj(t):
        y = jnp.dot(yc_ref[rows(t), :], wo_ref[0:CONV_WIDTH], preferred_element_type=F32)
        return y + jnp.dot(ya_ref[rows(t), :], wo_ref[CONV_WIDTH:], preferred_element_type=F32)

    def mid_norms(t, y):
        x = x_ref[rows(t), :] + _rms(y, g_post_mix)
        return x, _rms(x, g_pre_ffn).astype(BF16)

    def ffn_chunk(hb, c):
        sl = slice(c * FF_CHUNK, (c + 1) * FF_CHUNK)
        sl_up = slice(D_FF + c * FF_CHUNK, D_FF + (c + 1) * FF_CHUNK)
        gate = jnp.dot(hb, wi_ref[:, sl], preferred_element_type=F32)
        up = jnp.dot(hb, wi_ref[:, sl_up], preferred_element_type=F32)
        a = ((gate * jax.nn.sigmoid(gate)) * up).astype(BF16)
        return jnp.dot(a, wd_ref[sl, :], preferred_element_type=F32)

    def finish(t, x, f):
        o_ref[rows(t), :] = x + _rms(f, g_post_ffn)

    ys = [out_proj(t) for t in range(n_sub)]
    x, hb = mid_norms(0, ys[0])
    done = None
    for t in range(n_sub):
        f = None
        nxt = None
        for c in range(n_chunks):
            d = ffn_chunk(hb, c)
            f = d if f is None else f + d
            if c == 0 and done is not None:
                finish(*done)
            if c == n_chunks // 2 and t + 1 < n_sub:
                nxt = mid_norms(t + 1, ys[t + 1])
        done = (t, x, f)
        if nxt is not None:
            x, hb = nxt
    finish(*done)


def _outproj_ffn(x, yc, ya, layer, wo, gpm, gpf, wi, wd, gqf):
    t, d = x.shape
    tm = TM_FFN
    row = lambda i: (i, 0)
    return pl.pallas_call(
        functools.partial(_ffn_kernel, layer=layer),
        grid=(t // tm,),
        in_specs=[
            pl.BlockSpec((tm, d), row),
            pl.BlockSpec((tm, CONV_WIDTH), row),
            pl.BlockSpec((tm, ATTN_WIDTH), row),
            _whole_spec(wo, 1),
            _whole_spec(gpm, 1),
            _whole_spec(gpf, 1),
            _whole_spec(wi, 1),
            _whole_spec(wd, 1),
            _whole_spec(gqf, 1),
        ],
        out_specs=pl.BlockSpec((tm, d), row),
        out_shape=jax.ShapeDtypeStruct((t, d), F32),
        compiler_params=pltpu.CompilerParams(
            dimension_semantics=("arbitrary",), vmem_limit_bytes=VMEM_LIMIT),
        name="outproj_ffn",
    )(x, yc, ya, wo, gpm, gpf, wi, wd, gqf)


def kernel(x, w_in, w_conv, rel_bias, g_conv_out, g_attn_out, w_out, g_pre_mix, g_post_mix,
           g_pre_ffn, g_post_ffn, w_ffn_in, w_ffn_out):
    b, s, d = x.shape
    depth = w_in.shape[0]
    tabs = _bias_tables(rel_bias)
    w_conv_t = jnp.swapaxes(w_conv, 1, 2)
    w_in_b = w_in[0].astype(BF16)
    for l in range(depth):
        casts = [(w_out, l), (w_ffn_in, l), (w_ffn_out, l)]
        if l + 1 < depth:
            casts.append((w_in, l + 1))
        (yc, q, k, v), cast = _inproj(x, l, g_pre_mix, w_in_b, w_conv_t, g_conv_out, casts)
        ya = _attention(q, k, v, l, tabs, g_attn_out)
        x = _outproj_ffn(
            x.reshape(b * s, d), yc.reshape(b * s, CONV_WIDTH), ya.reshape(b * s, ATTN_WIDTH), l,
            cast[0], g_post_mix, g_pre_ffn, cast[1], cast[2], g_post_ffn).reshape(b, s, d)
        if l + 1 < depth:
            w_in_b = cast[3]
    return x
```

```python
import functools
import math

import jax
import jax.numpy as jnp
import numpy as np
from jax import lax
from jax.experimental import pallas as pl
from jax.experimental.pallas import tpu as pltpu

D_MODEL = 1024
CHUNK = 64
N_LEFT_CHUNKS = 8
LEFT_PAD = N_LEFT_CHUNKS * CHUNK
CONV_WIDTH = 512
ATTN_WIDTH = 512
HEAD_DIM = 64
N_HEADS = 8
CONV_K = 3
REL_CLIP = 128
D_FF = 2816
EPS = 1e-6
NEG_INF = -1e30
LOG2E = math.log2(math.e)

LANES = 128
TM_PROJ = 1024
TM_FFN = 1024
SUB_ROWS = 512
TQ = 256
N_KBLK = LEFT_PAD // TQ + 1
NK = N_KBLK * TQ
Q_BLOCKS = 4
TQS = Q_BLOCKS * TQ
SCORE_LEAD = 3
BF16_ROW_TILE = 16
TABLE_WIDTH = 1024
FF_CHUNK = 256
VMEM_LIMIT = 56 * 1024 * 1024

F32 = jnp.float32
BF16 = jnp.bfloat16


def _rms(x, g):
    return (x * lax.rsqrt(jnp.mean(x * x, axis=-1, keepdims=True) + EPS)) * g


def _pair_group_norm(y, g):
    lane = lax.broadcasted_iota(jnp.int32, y.shape, 1)
    lo = lane < HEAD_DIM
    sq = y * y
    ms_lo = jnp.sum(jnp.where(lo, sq, 0.0), axis=-1, keepdims=True) * (1.0 / HEAD_DIM)
    ms_hi = jnp.sum(jnp.where(lo, 0.0, sq), axis=-1, keepdims=True) * (1.0 / HEAD_DIM)
    inv = jnp.where(lo, lax.rsqrt(ms_lo + EPS), lax.rsqrt(ms_hi + EPS))
    return (y * inv) * g


def _layer_spec(a, layer, n_grid, single_buffer=False):
    zeros = (0,) * (a.ndim - 1)
    index = {1: lambda i: (layer,) + zeros, 2: lambda bi, i: (layer,) + zeros}[n_grid]
    mode = pl.Buffered(1) if single_buffer else None
    return pl.BlockSpec((None,) + a.shape[1:], index, pipeline_mode=mode)


def _whole_spec(a, n_grid):
    index = {1: lambda i: (0, 0), 2: lambda bi, i: (0, 0)}[n_grid]
    return pl.BlockSpec(a.shape, index, pipeline_mode=pl.Buffered(1))


def _inproj_kernel(x_ref, g_ref, w_ref, wc_ref, gco_ref, yc_ref, q_ref, k_ref, v_ref, carry_ref, *, layer):
    @pl.when(pl.program_id(1) == 0)
    def _():
        carry_ref[...] = jnp.zeros_like(carry_ref)

    n_sub = x_ref.shape[1] // SUB_ROWS
    lo = lax.broadcasted_iota(jnp.int32, (SUB_ROWS, LANES), 1) < HEAD_DIM
    row8 = lax.broadcasted_iota(jnp.int32, (8, 1), 0)
    g_pre = g_ref[layer:layer + 1, :]
    g_conv = gco_ref[layer:layer + 1, :]

    def sub_tile(t, hb, prev):
        rows = slice(t * SUB_ROWS, (t + 1) * SUB_ROWS)

        def project(j):
            return jnp.dot(hb, w_ref[:, j * CONV_WIDTH:(j + 1) * CONV_WIDTH], preferred_element_type=F32)

        h = project(0)
        u = project(2) * h
        u1 = pltpu.roll(u, 1, 0)
        u2 = pltpu.roll(u, 2, 0)
        u1 = jnp.concatenate([jnp.where(row8 == 0, prev[7:8], u1[:8]), u1[8:]], axis=0)
        u2 = jnp.concatenate(
            [jnp.where(row8 == 0, prev[6:7], jnp.where(row8 == 1, prev[7:8], u2[:8])), u2[8:]], axis=0)
        conv = project(1) * (u2 * wc_ref[0:1] + u1 * wc_ref[1:2] + u * wc_ref[2:3])
        for j in range(CONV_WIDTH // LANES):
            sl = slice(j * LANES, (j + 1) * LANES)
            yc_ref[0, rows, sl] = _pair_group_norm(conv[:, sl], g_conv[:, sl]).astype(BF16)

        def head_groups(ref, val, fill):
            for pair in range(N_HEADS // 2):
                sl = slice(pair * LANES, (pair + 1) * LANES)
                first = slice(2 * pair * LANES, (2 * pair + 1) * LANES)
                second = slice((2 * pair + 1) * LANES, (2 * pair + 2) * LANES)
                ref[0, rows, first] = jnp.where(lo, val[:, sl], fill).astype(BF16)
                ref[0, rows, second] = jnp.where(lo, fill, val[:, sl]).astype(BF16)

        head_groups(q_ref, project(3) * (LOG2E * HEAD_DIM ** -0.5), 0.0)
        k_ref[0, rows, :] = project(4).astype(BF16)
        head_groups(v_ref, project(5), 1.0)
        return u[SUB_ROWS - 8:]

    prev = carry_ref[...]
    for t in range(n_sub):
        hb = _rms(x_ref[0, t * SUB_ROWS:(t + 1) * SUB_ROWS, :], g_pre).astype(BF16)
        prev = sub_tile(t, hb, prev)
    carry_ref[...] = prev


def _inproj(x, layer, g, w, wc, gco):
    b, s, d = x.shape
    tm = TM_PROJ
    row = lambda bi, i: (bi, i, 0)
    narrow = jax.ShapeDtypeStruct((b, s, CONV_WIDTH), BF16)
    wide = jax.ShapeDtypeStruct((b, s, N_HEADS * LANES), BF16)
    narrow_spec = pl.BlockSpec((1, tm, CONV_WIDTH), row)
    wide_spec = pl.BlockSpec((1, tm, N_HEADS * LANES), row)
    return pl.pallas_call(
        functools.partial(_inproj_kernel, layer=layer),
        grid=(b, s // tm),
        in_specs=[
            pl.BlockSpec((1, tm, d), row),
            _whole_spec(g, 2),
            _whole_spec(w, 2),
            _layer_spec(wc, layer, 2),
            _whole_spec(gco, 2),
        ],
        out_specs=[narrow_spec, wide_spec, narrow_spec, wide_spec],
        out_shape=[narrow, wide, narrow, wide],
        scratch_shapes=[pltpu.VMEM((8, CONV_WIDTH), F32)],
        compiler_params=pltpu.CompilerParams(
            dimension_semantics=("arbitrary", "arbitrary"), vmem_limit_bytes=VMEM_LIMIT),
        name="inproj_conv",
    )(x, g, w, wc, gco)


def _build_tables(f_ref, o_ref):
    qc = lax.broadcasted_iota(jnp.int32, (TQ, NK), 0) // CHUNK
    kc = lax.broadcasted_iota(jnp.int32, (TQ, NK), 1) // CHUNK
    valid = (kc >= qc) & (kc <= qc + N_LEFT_CHUNKS)
    for head in range(N_HEADS):
        f = jnp.broadcast_to(f_ref[head] - f_ref[head][:, 0:1], (TQ, TABLE_WIDTH))
        r = pltpu.roll(f, 0, 1, stride=1, stride_axis=0)[:, :NK]
        o_ref[head] = jnp.where(valid, r * LOG2E, NEG_INF)


def _bias_rows(rel_bias):
    depth, h, _ = rel_bias.shape
    m = jnp.arange(TABLE_WIDTH)
    m = jnp.where(m < NK, m, m - TABLE_WIDTH)
    idx = jnp.clip(LEFT_PAD - m, -REL_CLIP, REL_CLIP) + REL_CLIP
    return rel_bias.astype(F32)[:, :, idx].reshape(depth, h, 1, TABLE_WIDTH)


def _table_tile_classes():
    qi = np.arange(TQ)[:, None]
    kj = np.arange(NK)[None, :]
    valid = (kj // CHUNK >= qi // CHUNK) & (kj // CHUNK <= qi // CHUNK + N_LEFT_CHUNKS)
    far = (LEFT_PAD + qi - kj) >= REL_CLIP
    classes = {}
    for qc in range(TQ // CHUNK):
        for ct in range(NK // LANES):
            blk = (slice(qc * CHUNK, (qc + 1) * CHUNK), slice(ct * LANES, (ct + 1) * LANES))
            if not valid[blk].any():
                classes[qc, ct] = "dead"
            elif valid[blk].all() and far[blk].all():
                classes[qc, ct] = "zero"
            else:
                classes[qc, ct] = "add"
    return classes


TILE_CLASS = _table_tile_classes()


def _attn_kernel(q_ref, kp_ref, kc_ref, vp_ref, vc_ref, f_ref, g_ref, *rest, layer):
    n_cast = (len(rest) - 2) // 2
    o_ref = rest[n_cast]
    tab_ref = rest[-1]
    for src, dst in zip(rest[:n_cast], rest[n_cast + 1:-1]):
        dst[...] = src[...].astype(BF16)

    i = pl.program_id(1)

    @pl.when((pl.program_id(0) == 0) & (i == 0))
    def _():
        _build_tables(f_ref, tab_ref)

    lo = lax.broadcasted_iota(jnp.int32, (TQ, LANES), 1) < HEAD_DIM
    nt = (((1,), (1,)), ((), ()))
    g_attn = g_ref[layer:layer + 1, :]

    def attend(first_step):
        def key_parts(j):
            lo_row = (j + 1 - N_KBLK) * TQ
            parts = []
            if lo_row < 0 and not first_step:
                parts.append((kp_ref, vp_ref, LEFT_PAD + lo_row, LEFT_PAD))
            parts.append((kc_ref, vc_ref, max(lo_row, 0), (j + 1) * TQ))
            return parts

        def scores(unit):
            j, head = unit
            pair = slice(head // 2 * LANES, (head // 2 + 1) * LANES)
            hl = slice(head * LANES, (head + 1) * LANES)
            q = q_ref[0, j * TQ:(j + 1) * TQ, hl]
            s = jnp.concatenate(
                [lax.dot_general(q, k[0, r0:r1, pair], nt, preferred_element_type=F32)
                 for k, _, r0, r1 in key_parts(j)], axis=1)
            ct0 = (NK - s.shape[1]) // LANES
            out = []
            for qc in range(TQ // CHUNK):
                rows = slice(qc * CHUNK, (qc + 1) * CHUNK)
                tiles = []
                for ct in range(ct0, NK // LANES):
                    cls = TILE_CLASS[qc, ct]
                    if cls == "dead":
                        continue
                    tile = s[rows, (ct - ct0) * LANES:(ct - ct0 + 1) * LANES]
                    if cls == "add":
                        tile = tile + tab_ref[head, rows, ct * LANES:(ct + 1) * LANES]
                    tiles.append((ct, tile))
                mx = jnp.max(functools.reduce(jnp.maximum, [t for _, t in tiles]), axis=-1, keepdims=True)
                out.append((tiles, mx))
            return ct0, out

        def weighted_values(unit, ct0, chunks):
            j, head = unit
            hl = slice(head * LANES, (head + 1) * LANES)
            p_rows = []
            for tiles, mx in chunks:
                live = dict(tiles)
                p_rows.append(jnp.concatenate(
                    [jnp.exp2(live[ct] - mx).astype(BF16) if ct in live else jnp.zeros((CHUNK, LANES), BF16)
                     for ct in range(ct0, NK // LANES)], axis=1))
            p = jnp.concatenate(p_rows, axis=0)
            acc, col = None, 0
            for _, v, r0, r1 in key_parts(j):
                part = jnp.dot(p[:, col:col + r1 - r0], v[0, r0:r1, hl], preferred_element_type=F32)
                acc = part if acc is None else acc + part
                col += r1 - r0
            return acc

        units = [(j, h) for j in range(Q_BLOCKS) for h in range(N_HEADS)]
        pending = [scores(u) for u in units[:SCORE_LEAD]]
        acc = []
        for n, unit in enumerate(units):
            if n + SCORE_LEAD < len(units):
                pending.append(scores(units[n + SCORE_LEAD]))
            acc.append(weighted_values(unit, *pending.pop(0)))
            j, head = unit
            if head % 2 == 1:
                sl = slice(head // 2 * LANES, (head // 2 + 1) * LANES)
                a0, a1 = acc
                acc = []
                y = jnp.where(lo, a0 / pltpu.roll(a0, HEAD_DIM, 1), a1 / pltpu.roll(a1, HEAD_DIM, 1))
                o_ref[0, j * TQ:(j + 1) * TQ, sl] = _pair_group_norm(y, g_attn[:, sl]).astype(BF16)

    @pl.when(i == 0)
    def _():
        attend(True)

    @pl.when(i > 0)
    def _():
        attend(False)


def _attention(q, k, v, layer, tab, g, casts):
    b, s, w = k.shape
    n_i = s // TQS
    n_steps = b * n_i
    cur = lambda bi, i: (bi, i, 0)
    prev = lambda bi, i: (bi, jnp.maximum(i * (TQS // LEFT_PAD) - 1, 0), 0)
    cast_in, cast_out, cast_shape = [], [], []
    for a, lyr in casts:
        _, r, c = a.shape
        per = 1
        while r * per % (n_steps * BF16_ROW_TILE):
            per *= 2
        rows = r * per // n_steps
        cast_in.append(pl.BlockSpec(
            (None, rows, c), lambda bi, i, lyr=lyr, per=per: (lyr, (bi * n_i + i) // per, 0)))
        cast_out.append(pl.BlockSpec((rows, c), lambda bi, i, per=per: ((bi * n_i + i) // per, 0)))
        cast_shape.append(jax.ShapeDtypeStruct((r, c), BF16))
    out = pl.pallas_call(
        functools.partial(_attn_kernel, layer=layer),
        grid=(b, n_i),
        in_specs=[
            pl.BlockSpec((1, TQS, q.shape[-1]), cur),
            pl.BlockSpec((1, LEFT_PAD, w), prev),
            pl.BlockSpec((1, TQS, w), cur),
            pl.BlockSpec((1, LEFT_PAD, v.shape[-1]), prev),
            pl.BlockSpec((1, TQS, v.shape[-1]), cur),
            _layer_spec(tab, layer, 2),
            _whole_spec(g, 2),
        ] + cast_in,
        out_specs=[pl.BlockSpec((1, TQS, w), cur)] + cast_out,
        out_shape=[jax.ShapeDtypeStruct((b, s, w), BF16)] + cast_shape,
        scratch_shapes=[pltpu.VMEM((N_HEADS, TQ, NK), F32)],
        compiler_params=pltpu.CompilerParams(
            dimension_semantics=("arbitrary", "arbitrary"), vmem_limit_bytes=VMEM_LIMIT),
        name="band_attention",
    )(q, k, k, v, v, tab, g, *[a for a, _ in casts])
    return out[0], out[1:]


def _ffn_kernel(x_ref, yc_ref, ya_ref, wo_ref, gpm_ref, gpf_ref, wi_ref, wd_ref, gqf_ref, o_ref, *, layer):
    n_sub = x_ref.shape[0] // SUB_ROWS
    n_chunks = D_FF // FF_CHUNK
    g_post_mix = gpm_ref[layer:layer + 1, :]
    g_pre_ffn = gpf_ref[layer:layer + 1, :]
    g_post_ffn = gqf_ref[layer:layer + 1, :]

    def rows(t):
        return slice(t * SUB_ROWS, (t + 1) * SUB_ROWS)

    def out_proj(t):
        y = jnp.dot(yc_ref[rows(t), :], wo_ref[0:CONV_WIDTH], preferred_element_type=F32)
        return y + jnp.dot(ya_ref[rows(t), :], wo_ref[CONV_WIDTH:], preferred_element_type=F32)

    def mid_norms(t, y):
        x = x_ref[rows(t), :] + _rms(y, g_post_mix)
        return x, _rms(x, g_pre_ffn).astype(BF16)

    def ffn_chunk(hb, c):
        sl = slice(c * FF_CHUNK, (c + 1) * FF_CHUNK)
        sl_up = slice(D_FF + c * FF_CHUNK, D_FF + (c + 1) * FF_CHUNK)
        gate = jnp.dot(hb, wi_ref[:, sl], preferred_element_type=F32)
        up = jnp.dot(hb, wi_ref[:, sl_up], preferred_element_type=F32)
        a = ((gate * jax.nn.sigmoid(gate)) * up).astype(BF16)
        return jnp.dot(a, wd_ref[sl, :], preferred_element_type=F32)

    def finish(t, x, f):
        o_ref[rows(t), :] = x + _rms(f, g_post_ffn)

    ys = [out_proj(t) for t in range(n_sub)]
    x, hb = mid_norms(0, ys[0])
    done = None
    for t in range(n_sub):
        f = None
        nxt = None
        for c in range(n_chunks):
            d = ffn_chunk(hb, c)
            f = d if f is None else f + d
            if c == 0 and done is not None:
                finish(*done)
            if c == n_chunks // 2 and t + 1 < n_sub:
                nxt = mid_norms(t + 1, ys[t + 1])
        done = (t, x, f)
        if nxt is not None:
            x, hb = nxt
    finish(*done)


def _outproj_ffn(x, yc, ya, layer, wo, gpm, gpf, wi, wd, gqf):
    t, d = x.shape
    tm = TM_FFN
    row = lambda i: (i, 0)
    return pl.pallas_call(
        functools.partial(_ffn_kernel, layer=layer),
        grid=(t // tm,),
        in_specs=[
            pl.BlockSpec((tm, d), row),
            pl.BlockSpec((tm, CONV_WIDTH), row),
            pl.BlockSpec((tm, ATTN_WIDTH), row),
            _whole_spec(wo, 1),
            _whole_spec(gpm, 1),
            _whole_spec(gpf, 1),
            _whole_spec(wi, 1),
            _whole_spec(wd, 1),
            _whole_spec(gqf, 1),
        ],
        out_specs=pl.BlockSpec((tm, d), row),
        out_shape=jax.ShapeDtypeStruct((t, d), F32),
        compiler_params=pltpu.CompilerParams(
            dimension_semantics=("arbitrary",), vmem_limit_bytes=VMEM_LIMIT),
        name="outproj_ffn",
    )(x, yc, ya, wo, gpm, gpf, wi, wd, gqf)


def kernel(x, w_in, w_conv, rel_bias, g_conv_out, g_attn_out, w_out, g_pre_mix, g_post_mix,
           g_pre_ffn, g_post_ffn, w_ffn_in, w_ffn_out):
    b, s, d = x.shape
    depth = w_in.shape[0]
    tabs = _bias_rows(rel_bias)
    w_conv_t = jnp.swapaxes(w_conv, 1, 2)
    w_in_b = w_in[0].astype(BF16)
    for l in range(depth):
        yc, q, k, v = _inproj(x, l, g_pre_mix, w_in_b, w_conv_t, g_conv_out)
        casts = [(w_out, l), (w_ffn_in, l), (w_ffn_out, l)]
        if l + 1 < depth:
            casts.append((w_in, l + 1))
        ya, cast = _attention(q, k, v, l, tabs, g_attn_out, casts)
        x = _outproj_ffn(
            x.reshape(b * s, d), yc.reshape(b * s, CONV_WIDTH), ya.reshape(b * s, ATTN_WIDTH), l,
            cast[0], g_post_mix, g_pre_ffn, cast[1], cast[2], g_post_ffn).reshape(b, s, d)
        if l + 1 < depth:
            w_in_b = cast[3]
    return x
```

```python
import functools
import math

import jax
import jax.numpy as jnp
import numpy as np
from jax import lax
from jax.experimental import pallas as pl
from jax.experimental.pallas import tpu as pltpu

D_MODEL = 1024
CHUNK = 64
N_LEFT_CHUNKS = 8
LEFT_PAD = N_LEFT_CHUNKS * CHUNK
CONV_WIDTH = 512
ATTN_WIDTH = 512
HEAD_DIM = 64
N_HEADS = 8
CONV_K = 3
REL_CLIP = 128
D_FF = 2816
EPS = 1e-6
NEG_INF = -1e30
LOG2E = math.log2(math.e)

LANES = 128
TM_PROJ = 1024
TM_FFN = 1024
SUB_ROWS = 512
TQ = 256
N_KBLK = LEFT_PAD // TQ + 1
NK = N_KBLK * TQ
Q_BLOCKS = 4
TQS = Q_BLOCKS * TQ
SCORE_LEAD = 3
BF16_ROW_TILE = 16
TABLE_WIDTH = 1024
FF_CHUNK = 256
VMEM_LIMIT = 56 * 1024 * 1024

F32 = jnp.float32
BF16 = jnp.bfloat16


def _rms(x, g):
    return (x * lax.rsqrt(jnp.mean(x * x, axis=-1, keepdims=True) + EPS)) * g


def _pair_group_norm(y, g):
    lane = lax.broadcasted_iota(jnp.int32, y.shape, 1)
    lo = lane < HEAD_DIM
    sq = y * y
    ms_lo = jnp.sum(jnp.where(lo, sq, 0.0), axis=-1, keepdims=True) * (1.0 / HEAD_DIM)
    ms_hi = jnp.sum(jnp.where(lo, 0.0, sq), axis=-1, keepdims=True) * (1.0 / HEAD_DIM)
    inv = jnp.where(lo, lax.rsqrt(ms_lo + EPS), lax.rsqrt(ms_hi + EPS))
    return (y * inv) * g


def _layer_spec(a, layer, n_grid, single_buffer=False):
    zeros = (0,) * (a.ndim - 1)
    index = {1: lambda i: (layer,) + zeros, 2: lambda bi, i: (layer,) + zeros}[n_grid]
    mode = pl.Buffered(1) if single_buffer else None
    return pl.BlockSpec((None,) + a.shape[1:], index, pipeline_mode=mode)


def _whole_spec(a, n_grid):
    index = {1: lambda i: (0, 0), 2: lambda bi, i: (0, 0)}[n_grid]
    return pl.BlockSpec(a.shape, index, pipeline_mode=pl.Buffered(1))


def _inproj_kernel(x_ref, g_ref, w_ref, wc_ref, gco_ref, yc_ref, q_ref, k_ref, v_ref, carry_ref, *w_cast,
                   layer):
    @pl.when(pl.program_id(1) == 0)
    def _():
        carry_ref[...] = jnp.zeros_like(carry_ref)

    if w_cast:
        w_b, = w_cast

        @pl.when((pl.program_id(0) == 0) & (pl.program_id(1) == 0))
        def _():
            w_b[...] = w_ref[...].astype(BF16)
    else:
        w_b = w_ref

    n_sub = x_ref.shape[1] // SUB_ROWS
    lo = lax.broadcasted_iota(jnp.int32, (SUB_ROWS, LANES), 1) < HEAD_DIM
    row8 = lax.broadcasted_iota(jnp.int32, (8, 1), 0)
    g_pre = g_ref[layer:layer + 1, :]
    g_conv = gco_ref[layer:layer + 1, :]

    def sub_tile(t, hb, prev):
        rows = slice(t * SUB_ROWS, (t + 1) * SUB_ROWS)

        def project(j):
            return jnp.dot(hb, w_b[:, j * CONV_WIDTH:(j + 1) * CONV_WIDTH], preferred_element_type=F32)

        h = project(0)
        u = project(2) * h
        u1 = pltpu.roll(u, 1, 0)
        u2 = pltpu.roll(u, 2, 0)
        u1 = jnp.concatenate([jnp.where(row8 == 0, prev[7:8], u1[:8]), u1[8:]], axis=0)
        u2 = jnp.concatenate(
            [jnp.where(row8 == 0, prev[6:7], jnp.where(row8 == 1, prev[7:8], u2[:8])), u2[8:]], axis=0)
        conv = project(1) * (u2 * wc_ref[0:1] + u1 * wc_ref[1:2] + u * wc_ref[2:3])
        for j in range(CONV_WIDTH // LANES):
            sl = slice(j * LANES, (j + 1) * LANES)
            yc_ref[0, rows, sl] = _pair_group_norm(conv[:, sl], g_conv[:, sl]).astype(BF16)

        def head_groups(ref, val, fill):
            for pair in range(N_HEADS // 2):
                sl = slice(pair * LANES, (pair + 1) * LANES)
                first = slice(2 * pair * LANES, (2 * pair + 1) * LANES)
                second = slice((2 * pair + 1) * LANES, (2 * pair + 2) * LANES)
                ref[0, rows, first] = jnp.where(lo, val[:, sl], fill).astype(BF16)
                ref[0, rows, second] = jnp.where(lo, fill, val[:, sl]).astype(BF16)

        head_groups(q_ref, project(3) * (LOG2E * HEAD_DIM ** -0.5), 0.0)
        k_ref[0, rows, :] = project(4).astype(BF16)
        head_groups(v_ref, project(5), 1.0)
        return u[SUB_ROWS - 8:]

    prev = carry_ref[...]
    for t in range(n_sub):
        hb = _rms(x_ref[0, t * SUB_ROWS:(t + 1) * SUB_ROWS, :], g_pre).astype(BF16)
        prev = sub_tile(t, hb, prev)
    carry_ref[...] = prev


def _inproj(x, layer, g, w, wc, gco):
    b, s, d = x.shape
    tm = TM_PROJ
    row = lambda bi, i: (bi, i, 0)
    narrow = jax.ShapeDtypeStruct((b, s, CONV_WIDTH), BF16)
    wide = jax.ShapeDtypeStruct((b, s, N_HEADS * LANES), BF16)
    narrow_spec = pl.BlockSpec((1, tm, CONV_WIDTH), row)
    wide_spec = pl.BlockSpec((1, tm, N_HEADS * LANES), row)
    stacked_f32 = w.ndim == 3
    w_spec = _layer_spec(w, layer, 2, single_buffer=True) if stacked_f32 else _whole_spec(w, 2)
    w_scratch = [pltpu.VMEM(w.shape[1:], BF16)] if stacked_f32 else []
    return pl.pallas_call(
        functools.partial(_inproj_kernel, layer=layer),
        grid=(b, s // tm),
        in_specs=[
            pl.BlockSpec((1, tm, d), row),
            _whole_spec(g, 2),
            w_spec,
            _layer_spec(wc, layer, 2),
            _whole_spec(gco, 2),
        ],
        out_specs=[narrow_spec, wide_spec, narrow_spec, wide_spec],
        out_shape=[narrow, wide, narrow, wide],
        scratch_shapes=[pltpu.VMEM((8, CONV_WIDTH), F32)] + w_scratch,
        compiler_params=pltpu.CompilerParams(
            dimension_semantics=("arbitrary", "arbitrary"), vmem_limit_bytes=VMEM_LIMIT),
        name="inproj_conv",
    )(x, g, w, wc, gco)


def _build_tables(f_ref, o_ref):
    qc = lax.broadcasted_iota(jnp.int32, (TQ, NK), 0) // CHUNK
    kc = lax.broadcasted_iota(jnp.int32, (TQ, NK), 1) // CHUNK
    valid = (kc >= qc) & (kc <= qc + N_LEFT_CHUNKS)
    for head in range(N_HEADS):
        f = jnp.broadcast_to(f_ref[head] - f_ref[head][:, 0:1], (TQ, TABLE_WIDTH))
        r = pltpu.roll(f, 0, 1, stride=1, stride_axis=0)[:, :NK]
        o_ref[head] = jnp.where(valid, r * LOG2E, NEG_INF)


def _bias_rows(rel_bias):
    depth, h, _ = rel_bias.shape
    m = jnp.arange(TABLE_WIDTH)
    m = jnp.where(m < NK, m, m - TABLE_WIDTH)
    idx = jnp.clip(LEFT_PAD - m, -REL_CLIP, REL_CLIP) + REL_CLIP
    return rel_bias.astype(F32)[:, :, idx].reshape(depth, h, 1, TABLE_WIDTH)


def _table_tile_classes():
    qi = np.arange(TQ)[:, None]
    kj = np.arange(NK)[None, :]
    valid = (kj // CHUNK >= qi // CHUNK) & (kj // CHUNK <= qi // CHUNK + N_LEFT_CHUNKS)
    far = (LEFT_PAD + qi - kj) >= REL_CLIP
    classes = {}
    for qc in range(TQ // CHUNK):
        for ct in range(NK // LANES):
            blk = (slice(qc * CHUNK, (qc + 1) * CHUNK), slice(ct * LANES, (ct + 1) * LANES))
            if not valid[blk].any():
                classes[qc, ct] = "dead"
            elif valid[blk].all() and far[blk].all():
                classes[qc, ct] = "zero"
            else:
                classes[qc, ct] = "add"
    return classes


TILE_CLASS = _table_tile_classes()


def _attn_kernel(q_ref, kp_ref, kc_ref, vp_ref, vc_ref, f_ref, g_ref, *rest, layer):
    n_cast = (len(rest) - 2) // 2
    o_ref = rest[n_cast]
    tab_ref = rest[-1]
    for src, dst in zip(rest[:n_cast], rest[n_cast + 1:-1]):
        dst[...] = src[...].astype(BF16)

    i = pl.program_id(1)

    @pl.when((pl.program_id(0) == 0) & (i == 0))
    def _():
        _build_tables(f_ref, tab_ref)

    lo = lax.broadcasted_iota(jnp.int32, (TQ, LANES), 1) < HEAD_DIM
    nt = (((1,), (1,)), ((), ()))
    g_attn = g_ref[layer:layer + 1, :]

    def attend(first_step):
        def key_parts(j):
            lo_row = (j + 1 - N_KBLK) * TQ
            parts = []
            if lo_row < 0 and not first_step:
                parts.append((kp_ref, vp_ref, LEFT_PAD + lo_row, LEFT_PAD))
            parts.append((kc_ref, vc_ref, max(lo_row, 0), (j + 1) * TQ))
            return parts

        def scores(unit):
            j, head = unit
            pair = slice(head // 2 * LANES, (head // 2 + 1) * LANES)
            hl = slice(head * LANES, (head + 1) * LANES)
            q = q_ref[0, j * TQ:(j + 1) * TQ, hl]
            s = jnp.concatenate(
                [lax.dot_general(q, k[0, r0:r1, pair], nt, preferred_element_type=F32)
                 for k, _, r0, r1 in key_parts(j)], axis=1)
            ct0 = (NK - s.shape[1]) // LANES
            out = []
            for qc in range(TQ // CHUNK):
                rows = slice(qc * CHUNK, (qc + 1) * CHUNK)
                tiles = []
                for ct in range(ct0, NK // LANES):
                    cls = TILE_CLASS[qc, ct]
                    if cls == "dead":
                        continue
                    tile = s[rows, (ct - ct0) * LANES:(ct - ct0 + 1) * LANES]
                    if cls == "add":
                        tile = tile + tab_ref[head, rows, ct * LANES:(ct + 1) * LANES]
                    tiles.append((ct, tile))
                mx = jnp.max(functools.reduce(jnp.maximum, [t for _, t in tiles]), axis=-1, keepdims=True)
                out.append((tiles, mx))
            return ct0, out

        def weighted_values(unit, ct0, chunks):
            j, head = unit
            hl = slice(head * LANES, (head + 1) * LANES)
            p_rows = []
            for tiles, mx in chunks:
                live = dict(tiles)
                p_rows.append(jnp.concatenate(
                    [jnp.exp2(live[ct] - mx).astype(BF16) if ct in live else jnp.zeros((CHUNK, LANES), BF16)
                     for ct in range(ct0, NK // LANES)], axis=1))
            p = jnp.concatenate(p_rows, axis=0)
            acc, col = None, 0
            for _, v, r0, r1 in key_parts(j):
                part = jnp.dot(p[:, col:col + r1 - r0], v[0, r0:r1, hl], preferred_element_type=F32)
                acc = part if acc is None else acc + part
                col += r1 - r0
            return acc

        units = [(j, h) for j in range(Q_BLOCKS) for h in range(N_HEADS)]
        pending = [scores(u) for u in units[:SCORE_LEAD]]
        acc = []
        for n, unit in enumerate(units):
            if n + SCORE_LEAD < len(units):
                pending.append(scores(units[n + SCORE_LEAD]))
            acc.append(weighted_values(unit, *pending.pop(0)))
            j, head = unit
            if head % 2 == 1:
                sl = slice(head // 2 * LANES, (head // 2 + 1) * LANES)
                a0, a1 = acc
                acc = []
                y = jnp.where(lo, a0 / pltpu.roll(a0, HEAD_DIM, 1), a1 / pltpu.roll(a1, HEAD_DIM, 1))
                o_ref[0, j * TQ:(j + 1) * TQ, sl] = _pair_group_norm(y, g_attn[:, sl]).astype(BF16)

    @pl.when(i == 0)
    def _():
        attend(True)

    @pl.when(i > 0)
    def _():
        attend(False)


def _attention(q, k, v, layer, tab, g, casts):
    b, s, w = k.shape
    n_i = s // TQS
    n_steps = b * n_i
    cur = lambda bi, i: (bi, i, 0)
    prev = lambda bi, i: (bi, jnp.maximum(i * (TQS // LEFT_PAD) - 1, 0), 0)
    cast_in, cast_out, cast_shape = [], [], []
    for a, lyr in casts:
        _, r, c = a.shape
        per = 1
        while r * per % (n_steps * BF16_ROW_TILE):
            per *= 2
        rows = r * per // n_steps
        cast_in.append(pl.BlockSpec(
            (None, rows, c), lambda bi, i, lyr=lyr, per=per: (lyr, (bi * n_i + i) // per, 0)))
        cast_out.append(pl.BlockSpec((rows, c), lambda bi, i, per=per: ((bi * n_i + i) // per, 0)))
        cast_shape.append(jax.ShapeDtypeStruct((r, c), BF16))
    out = pl.pallas_call(
        functools.partial(_attn_kernel, layer=layer),
        grid=(b, n_i),
        in_specs=[
            pl.BlockSpec((1, TQS, q.shape[-1]), cur),
            pl.BlockSpec((1, LEFT_PAD, w), prev),
            pl.BlockSpec((1, TQS, w), cur),
            pl.BlockSpec((1, LEFT_PAD, v.shape[-1]), prev),
            pl.BlockSpec((1, TQS, v.shape[-1]), cur),
            _layer_spec(tab, layer, 2),
            _whole_spec(g, 2),
        ] + cast_in,
        out_specs=[pl.BlockSpec((1, TQS, w), cur)] + cast_out,
        out_shape=[jax.ShapeDtypeStruct((b, s, w), BF16)] + cast_shape,
        scratch_shapes=[pltpu.VMEM((N_HEADS, TQ, NK), F32)],
        compiler_params=pltpu.CompilerParams(
            dimension_semantics=("arbitrary", "arbitrary"), vmem_limit_bytes=VMEM_LIMIT),
        name="band_attention",
    )(q, k, k, v, v, tab, g, *[a for a, _ in casts])
    return out[0], out[1:]


def _ffn_kernel(x_ref, yc_ref, ya_ref, wo_ref, gpm_ref, gpf_ref, wi_ref, wd_ref, gqf_ref, o_ref, *, layer):
    n_sub = x_ref.shape[0] // SUB_ROWS
    n_chunks = D_FF // FF_CHUNK
    g_post_mix = gpm_ref[layer:layer + 1, :]
    g_pre_ffn = gpf_ref[layer:layer + 1, :]
    g_post_ffn = gqf_ref[layer:layer + 1, :]

    def rows(t):
        return slice(t * SUB_ROWS, (t + 1) * SUB_ROWS)

    def out_proj(t):
        y = jnp.dot(yc_ref[rows(t), :], wo_ref[0:CONV_WIDTH], preferred_element_type=F32)
        return y + jnp.dot(ya_ref[rows(t), :], wo_ref[CONV_WIDTH:], preferred_element_type=F32)

    def mid_norms(t, y):
        x = x_ref[rows(t), :] + _rms(y, g_post_mix)
        return x, _rms(x, g_pre_ffn).astype(BF16)

    def ffn_chunk(hb, c):
        sl = slice(c * FF_CHUNK, (c + 1) * FF_CHUNK)
        sl_up = slice(D_FF + c * FF_CHUNK, D_FF + (c + 1) * FF_CHUNK)
        gate = jnp.dot(hb, wi_ref[:, sl], preferred_element_type=F32)
        up = jnp.dot(hb, wi_ref[:, sl_up], preferred_element_type=F32)
        a = ((gate * jax.nn.sigmoid(gate)) * up).astype(BF16)
        return jnp.dot(a, wd_ref[sl, :], preferred_element_type=F32)

    def finish(t, x, f):
        o_ref[rows(t), :] = x + _rms(f, g_post_ffn)

    ys = [out_proj(t) for t in range(n_sub)]
    x, hb = mid_norms(0, ys[0])
    done = None
    for t in range(n_sub):
        f = None
        nxt = None
        for c in range(n_chunks):
            d = ffn_chunk(hb, c)
            f = d if f is None else f + d
            if c == 0 and done is not None:
                finish(*done)
            if c == n_chunks // 2 and t + 1 < n_sub:
                nxt = mid_norms(t + 1, ys[t + 1])
        done = (t, x, f)
        if nxt is not None:
            x, hb = nxt
    finish(*done)


def _outproj_ffn(x, yc, ya, layer, wo, gpm, gpf, wi, wd, gqf):
    t, d = x.shape
    tm = TM_FFN
    row = lambda i: (i, 0)
    return pl.pallas_call(
        functools.partial(_ffn_kernel, layer=layer),
        grid=(t // tm,),
        in_specs=[
            pl.BlockSpec((tm, d), row),
            pl.BlockSpec((tm, CONV_WIDTH), row),
            pl.BlockSpec((tm, ATTN_WIDTH), row),
            _whole_spec(wo, 1),
            _whole_spec(gpm, 1),
            _whole_spec(gpf, 1),
            _whole_spec(wi, 1),
            _whole_spec(wd, 1),
            _whole_spec(gqf, 1),
        ],
        out_specs=pl.BlockSpec((tm, d), row),
        out_shape=jax.ShapeDtypeStruct((t, d), F32),
        compiler_params=pltpu.CompilerParams(
            dimension_semantics=("arbitrary",), vmem_limit_bytes=VMEM_LIMIT),
        name="outproj_ffn",
    )(x, yc, ya, wo, gpm, gpf, wi, wd, gqf)


def kernel(x, w_in, w_conv, rel_bias, g_conv_out, g_attn_out, w_out, g_pre_mix, g_post_mix,
           g_pre_ffn, g_post_ffn, w_ffn_in, w_ffn_out):
    b, s, d = x.shape
    depth = w_in.shape[0]
    tabs = _bias_rows(rel_bias)
    w_conv_t = jnp.swapaxes(w_conv, 1, 2)
    w_in_b = w_in
    for l in range(depth):
        yc, q, k, v = _inproj(x, l, g_pre_mix, w_in_b, w_conv_t, g_conv_out)
        casts = [(w_out, l), (w_ffn_in, l), (w_ffn_out, l)]
        if l + 1 < depth:
            casts.append((w_in, l + 1))
        ya, cast = _attention(q, k, v, l, tabs, g_attn_out, casts)
        x = _outproj_ffn(
            x.reshape(b * s, d), yc.reshape(b * s, CONV_WIDTH), ya.reshape(b * s, ATTN_WIDTH), l,
            cast[0], g_post_mix, g_pre_ffn, cast[1], cast[2], g_post_ffn).reshape(b, s, d)
        if l + 1 < depth:
            w_in_b = cast[3]
    return x
```
